```python
import math
import jax
import jax.numpy as jnp
from jax import lax
import numpy as np

D_MODEL = 2048
BATCH = 1
SEQ = 16384
DEPTH = 2

GRID_W = 64
CTX_LEN = 256

S5_WIDTH = 1024
S5_GROUP = 16
S5_GROUPS = S5_WIDTH // S5_GROUP
S5_STATE = 64

SSD_WIDTH = 2048
SSD_HEAD_DIM = 64
SSD_HEADS = SSD_WIDTH // SSD_HEAD_DIM
SSD_GROUPS = 4
SSD_STATE = 128
SSD_CONV = 3
SSD_CHUNK = 128
SSD_CONV_CH = SSD_WIDTH + 2 * SSD_GROUPS * SSD_STATE

D_FF = 5632
FFN_CONV = 3

COL_XBC = S5_WIDTH
COL_DT = COL_XBC + SSD_CONV_CH
COL_Z = COL_DT + 2 * SSD_HEADS
COL_GATES = COL_Z + SSD_WIDTH
PROJ_COLS = COL_GATES + 2 * D_MODEL

DEEPNORM_ALPHA = (2 * DEPTH) ** 0.25
DEEPNORM_BETA = (8 * DEPTH) ** -0.25
LN_EPS = 1e-5
RMS_EPS = 1e-5
F32 = jnp.float32

kernel_name = 'hybrid_s5_ssd_convffn_diffusion_block'


def layer_norm(x, g, b):
    xf = x.astype(F32)
    mu = jnp.mean(xf, axis=-1, keepdims=True)
    var = jnp.mean(jnp.square(xf - mu), axis=-1, keepdims=True)
    return ((xf - mu) * lax.rsqrt(var + LN_EPS) * g.astype(F32) + b.astype(F32)).astype(x.dtype)


def modulate(h, shift, scale):
    return h * (1.0 + scale) + shift


def adaln(cvec, w, b):
    return jax.nn.silu(cvec) @ w + b


def _ident(a):
    return a


def _rev(a):
    return jnp.flip(a, axis=1)


def dwconv_seq(x, w, b):
    k, ch = w.shape
    y = lax.conv_general_dilated(x, w[:, None, :], window_strides=(1,), padding=[(k // 2, k // 2)],
                                 dimension_numbers=('NWC', 'WIO', 'NWC'), feature_group_count=ch)
    return y + b


def dwconv_grid(x, w, b):
    bsz, n, ch = x.shape
    rows = n // GRID_W
    k = w.shape[0]
    img = x.reshape(bsz, rows, GRID_W, ch)
    y = lax.conv_general_dilated(img, w[:, :, None, :], window_strides=(1, 1),
                                 padding=[(k // 2, k // 2), (k // 2, k // 2)],
                                 dimension_numbers=('NHWC', 'HWIO', 'NHWC'), feature_group_count=ch)
    return (y + b).reshape(bsz, n, ch)


def s5_discretise(lam_re, lam_im, log_step, b_re, b_im):
    lam_re = jnp.minimum(lam_re.astype(F32), -1e-4)
    lam_im = lam_im.astype(F32)
    step = jnp.exp(log_step.astype(F32))[:, None]
    mag = jnp.exp(lam_re * step)
    ab_re = mag * jnp.cos(lam_im * step)
    ab_im = mag * jnp.sin(lam_im * step)
    den = jnp.square(lam_re) + jnp.square(lam_im)
    q_re = ((ab_re - 1.0) * lam_re + ab_im * lam_im) / den
    q_im = (ab_im * lam_re - (ab_re - 1.0) * lam_im) / den
    b_re = b_re.astype(F32)
    b_im = b_im.astype(F32)
    bb_re = q_re[..., None] * b_re - q_im[..., None] * b_im
    bb_im = q_re[..., None] * b_im + q_im[..., None] * b_re
    return ab_re, ab_im, bb_re, bb_im


def _complex_affine_combine(e1, e2):
    a1r, a1i, b1r, b1i = e1
    a2r, a2i, b2r, b2i = e2
    return (a2r * a1r - a2i * a1i,
            a2r * a1i + a2i * a1r,
            a2r * b1r - a2i * b1i + b2r,
            a2r * b1i + a2i * b1r + b2i)


def s5_states(u, disc, s0=None):
    ab_re, ab_im, bb_re, bb_im = disc
    bu_re = jnp.einsum('blgj,gpj->blgp', u, bb_re)
    bu_im = jnp.einsum('blgj,gpj->blgp', u, bb_im)
    if s0 is not None:
        s0_re, s0_im = s0
        bu_re = bu_re.at[:, 0].add(ab_re * s0_re - ab_im * s0_im)
        bu_im = bu_im.at[:, 0].add(ab_re * s0_im + ab_im * s0_re)
    a_re = jnp.broadcast_to(ab_re, bu_re.shape)
    a_im = jnp.broadcast_to(ab_im, bu_im.shape)
    _, _, s_re, s_im = lax.associative_scan(_complex_affine_combine, (a_re, a_im, bu_re, bu_im), axis=1)
    return s_re, s_im


def s5_readout(s_re, s_im, c_re, c_im):
    return jnp.einsum('blgp,gjp->blgj', s_re, c_re.astype(F32)) - jnp.einsum('blgp,gjp->blgj', s_im, c_im.astype(F32))


def s5_glu(y, u, d, w_glu):
    bsz, n = y.shape[:2]
    y = (y + u * d.astype(F32).reshape(S5_GROUPS, S5_GROUP)).reshape(bsz, n, S5_WIDTH)
    a = jax.nn.gelu(y) @ w_glu
    return a[..., :S5_WIDTH] * jax.nn.sigmoid(a[..., S5_WIDTH:])


def s5_branch(u_lat, u_ctx, lam_re, lam_im, log_step, b_re, b_im, c_re, c_im, d, w_glu, ctx_out):
    bsz = u_lat.shape[0]
    ul = u_lat.astype(F32).reshape(bsz, u_lat.shape[1], S5_GROUPS, S5_GROUP)
    uc = u_ctx.astype(F32).reshape(bsz, u_ctx.shape[1], S5_GROUPS, S5_GROUP)
    ys_l, ys_c = [], []
    for k, order in ((0, _ident), (1, _rev)):
        disc = s5_discretise(lam_re[k], lam_im[k], log_step[k], b_re, b_im)
        sc_re, sc_im = s5_states(order(uc), disc)
        sl_re, sl_im = s5_states(order(ul), disc, (sc_re[:, -1], sc_im[:, -1]))
        ys_l.append(order(s5_readout(sl_re, sl_im, c_re[k], c_im[k])))
        if ctx_out:
            ys_c.append(order(s5_readout(sc_re, sc_im, c_re[k], c_im[k])))
    y_lat = s5_glu(ys_l[0] + ys_l[1], ul, d, w_glu).astype(u_lat.dtype)
    y_ctx = s5_glu(ys_c[0] + ys_c[1], uc, d, w_glu).astype(u_ctx.dtype) if ctx_out else None
    return y_lat, y_ctx


def segsum_exp(cum):
    t = cum.shape[-1]
    mask = jnp.tril(jnp.ones((t, t), dtype=bool))
    diff = cum[..., :, None] - cum[..., None, :]
    return jnp.where(mask, jnp.exp(jnp.where(mask, diff, 0.0)), 0.0)


def ssd_chunked(xh, dt, a, bm, cm, s0):
    bsz, n, _, hp = xh.shape
    grp, ns = bm.shape[2], bm.shape[3]
    rep = SSD_HEADS // grp
    nc = n // SSD_CHUNK
    xc = (xh * dt[..., None]).reshape(bsz, nc, SSD_CHUNK, grp, rep, hp)
    la = jnp.moveaxis((dt * a).reshape(bsz, nc, SSD_CHUNK, grp, rep), 2, -1)
    cum = jnp.cumsum(la, axis=-1)
    bc = bm.reshape(bsz, nc, SSD_CHUNK, grp, ns)
    cc = cm.reshape(bsz, nc, SSD_CHUNK, grp, ns)
    cb = jnp.einsum('bclgn,bcsgn->bcgls', cc, bc)
    scores = cb[:, :, :, None] * segsum_exp(cum)
    y = jnp.einsum('bcgrls,bcsgrp->bclgrp', scores, xc)
    decay_to_end = jnp.exp(cum[..., -1:] - cum)
    states = jnp.einsum('bclgn,bcgrl,bclgrp->bcgrpn', bc, decay_to_end, xc)
    states = jnp.concatenate([s0[:, None], states], axis=1)
    chunk_cum = jnp.cumsum(jnp.pad(cum[..., -1], ((0, 0), (1, 0), (0, 0), (0, 0))), axis=1)
    decay_chunk = segsum_exp(jnp.moveaxis(chunk_cum, 1, -1))
    states_in = jnp.einsum('bgrzc,bcgrpn->bzgrpn', decay_chunk[..., :-1, :], states)
    y = y + jnp.einsum('bclgn,bcgrpn,bcgrl->bclgrp', cc, states_in, jnp.exp(cum))
    return y.reshape(bsz, n, SSD_HEADS, hp)


def ssd_final_state(xh, dt, a, bm):
    bsz, n, _, hp = xh.shape
    grp = bm.shape[2]
    rep = SSD_HEADS // grp
    cum = jnp.cumsum(dt * a, axis=1)
    decay = jnp.exp(cum[:, -1:] - cum).reshape(bsz, n, grp, rep)
    xd = (xh * dt[..., None]).reshape(bsz, n, grp, rep, hp)
    return jnp.einsum('blgn,blgr,blgrp->bgrpn', bm, decay, xd)


def ssd_prepare(xbc, dtr, conv_w, conv_b, dt_bias):
    xbc = jax.nn.silu(dwconv_seq(xbc, conv_w, conv_b)).astype(F32)
    bsz, n, _ = xbc.shape
    gn = SSD_GROUPS * SSD_STATE
    xh = xbc[..., :SSD_WIDTH].reshape(bsz, n, SSD_HEADS, SSD_HEAD_DIM)
    bm = xbc[..., SSD_WIDTH:SSD_WIDTH + gn].reshape(bsz, n, SSD_GROUPS, SSD_STATE)
    cm = xbc[..., SSD_WIDTH + gn:].reshape(bsz, n, SSD_GROUPS, SSD_STATE)
    dt = jax.nn.softplus(dtr.astype(F32).reshape(bsz, n, 2, SSD_HEADS) + dt_bias.astype(F32))
    return xh, bm, cm, dt


def ssd_output(y, xh, z, d, norm_w):
    y = (y + xh * d.astype(F32)[:, None]).reshape(z.shape)
    h = y * jax.nn.silu(z.astype(F32))
    hg = h.reshape(h.shape[0], h.shape[1], SSD_GROUPS, SSD_WIDTH // SSD_GROUPS)
    hg = hg * lax.rsqrt(jnp.mean(jnp.square(hg), axis=-1, keepdims=True) + RMS_EPS)
    return (hg.reshape(h.shape) * norm_w.astype(F32)).astype(z.dtype)


def ssd_branch(xbc_l, dtr_l, z_l, xbc_c, dtr_c, z_c, conv_w, conv_b, dt_bias, a_log, d, norm_w):
    a = -jnp.exp(a_log.astype(F32))
    xl, bl, cl, dtl = ssd_prepare(xbc_l, dtr_l, conv_w, conv_b, dt_bias)
    xc, bc, cc, dtc = ssd_prepare(xbc_c, dtr_c, conv_w, conv_b, dt_bias)
    ys_l, ys_c = [], []
    for k, order in ((0, _ident), (1, _rev)):
        s_ctx = ssd_final_state(order(xc), order(dtc[:, :, k]), a[k], order(bc))
        ys_l.append(order(ssd_chunked(order(xl), order(dtl[:, :, k]), a[k], order(bl), order(cl), s_ctx)))
        if z_c is not None:
            ys_c.append(order(ssd_chunked(order(xc), order(dtc[:, :, k]), a[k], order(bc), order(cc),
                                          jnp.zeros_like(s_ctx))))
    y_lat = ssd_output(ys_l[0] + ys_l[1], xl, z_l, d, norm_w)
    y_ctx = ssd_output(ys_c[0] + ys_c[1], xc, z_c, d, norm_w) if z_c is not None else None
    return y_lat, y_ctx


def merge_branches(y_s5, y_ssd, gates, w_s5p, w_ssdp, w_out):
    g = jax.nn.sigmoid(gates.astype(F32)).astype(gates.dtype)
    merged = g[..., :D_MODEL] * (y_s5 @ w_s5p) + g[..., D_MODEL:] * (y_ssd @ w_ssdp)
    return merged @ w_out


def conv_ffn(h, w_up, conv_w, conv_b, w_down, grid):
    up = h @ w_up
    gate, val = up[..., :D_FF], up[..., D_FF:]
    gate = dwconv_grid(gate, conv_w, conv_b) if grid else dwconv_seq(gate, conv_w[FFN_CONV // 2], conv_b)
    return (jax.nn.silu(gate) * val) @ w_down


def setup_inputs(seed: int = 0) -> dict:
    key = jax.random.key(seed)
    kit = iter(jax.random.split(key, 48))

    def normal(shape, scale):
        return scale * jax.random.normal(next(kit), shape, F32)

    L = DEPTH
    x = normal((BATCH, SEQ, D_MODEL), 1.0)
    c = normal((BATCH, D_MODEL), 1.0)
    ctx = normal((BATCH, CTX_LEN, D_MODEL), 1.0)
    c_ctx = normal((D_MODEL,), 1.0)
    w_ada = normal((L, D_MODEL, 6 * D_MODEL), 0.5 * D_MODEL ** -0.5)
    b_ada = normal((L, 6 * D_MODEL), 0.01)
    w_in = normal((L, D_MODEL, PROJ_COLS), D_MODEL ** -0.5)
    n_idx = jnp.arange(S5_STATE, dtype=F32)
    s5_lam_re = -0.5 + normal((L, 2, S5_GROUPS, S5_STATE), 0.01)
    s5_lam_im = jnp.pi * n_idx + normal((L, 2, S5_GROUPS, S5_STATE), 0.01)
    s5_log_step = jax.random.uniform(next(kit), (L, 2, S5_GROUPS), F32, math.log(1e-3), math.log(1e-1))
    s5_b_re = normal((L, S5_GROUPS, S5_STATE, S5_GROUP), (2 * S5_GROUP) ** -0.5)
    s5_b_im = normal((L, S5_GROUPS, S5_STATE, S5_GROUP), (2 * S5_GROUP) ** -0.5)
    s5_c_re = normal((L, 2, S5_GROUPS, S5_GROUP, S5_STATE), 0.5)
    s5_c_im = normal((L, 2, S5_GROUPS, S5_GROUP, S5_STATE), 0.5)
    s5_d = normal((L, S5_WIDTH), 1.0)
    s5_w_glu = normal((L, S5_WIDTH, 2 * S5_WIDTH), S5_WIDTH ** -0.5)
    s5_w_proj = normal((L, S5_WIDTH, D_MODEL), DEEPNORM_BETA * S5_WIDTH ** -0.5)
    ssd_conv_w = normal((L, SSD_CONV, SSD_CONV_CH), SSD_CONV ** -0.5)
    ssd_conv_b = normal((L, SSD_CONV_CH), 0.01)
    dt0 = jnp.exp(jax.random.uniform(next(kit), (L, 2, SSD_HEADS), F32, math.log(1e-3), math.log(1e-1)))
    ssd_dt_bias = dt0 + jnp.log(-jnp.expm1(-dt0))
    ssd_a_log = jnp.log(jax.random.uniform(next(kit), (L, 2, SSD_HEADS), F32, 1.0, 16.0))
    ssd_d = 1.0 + normal((L, SSD_HEADS), 0.1)
    ssd_norm_w = 1.0 + normal((L, SSD_WIDTH), 0.01)
    ssd_w_proj = normal((L, SSD_WIDTH, D_MODEL), DEEPNORM_BETA * SSD_WIDTH ** -0.5)
    w_out = normal((L, D_MODEL, D_MODEL), DEEPNORM_BETA * D_MODEL ** -0.5)
    ln1_g = 1.0 + normal((L, D_MODEL), 0.01)
    ln1_b = normal((L, D_MODEL), 0.01)
    w_up = normal((L, D_MODEL, 2 * D_FF), D_MODEL ** -0.5)
    ffn_conv_w = normal((L, FFN_CONV, FFN_CONV, D_FF), 1.0 / FFN_CONV)
    ffn_conv_b = normal((L, D_FF), 0.01)
    w_down = normal((L, D_FF, D_MODEL), DEEPNORM_BETA * D_FF ** -0.5)
    ln2_g = 1.0 + normal((L, D_MODEL), 0.01)
    ln2_b = normal((L, D_MODEL), 0.01)
    return {'x': x, 'c': c, 'ctx': ctx, 'c_ctx': c_ctx, 'w_ada': w_ada, 'b_ada': b_ada, 'w_in': w_in,
            's5_lam_re': s5_lam_re, 's5_lam_im': s5_lam_im, 's5_log_step': s5_log_step,
            's5_b_re': s5_b_re, 's5_b_im': s5_b_im, 's5_c_re': s5_c_re, 's5_c_im': s5_c_im,
            's5_d': s5_d, 's5_w_glu': s5_w_glu, 's5_w_proj': s5_w_proj,
            'ssd_conv_w': ssd_conv_w, 'ssd_conv_b': ssd_conv_b, 'ssd_dt_bias': ssd_dt_bias,
            'ssd_a_log': ssd_a_log, 'ssd_d': ssd_d, 'ssd_norm_w': ssd_norm_w, 'ssd_w_proj': ssd_w_proj,
            'w_out': w_out, 'ln1_g': ln1_g, 'ln1_b': ln1_b, 'w_up': w_up, 'ffn_conv_w': ffn_conv_w,
            'ffn_conv_b': ffn_conv_b, 'w_down': w_down, 'ln2_g': ln2_g, 'ln2_b': ln2_b}


def reference(x, c, ctx, c_ctx, w_ada, b_ada, w_in, s5_lam_re, s5_lam_im, s5_log_step, s5_b_re, s5_b_im,
              s5_c_re, s5_c_im, s5_d, s5_w_glu, s5_w_proj, ssd_conv_w, ssd_conv_b, ssd_dt_bias, ssd_a_log,
              ssd_d, ssd_norm_w, ssd_w_proj, w_out, ln1_g, ln1_b, w_up, ffn_conv_w, ffn_conv_b, w_down,
              ln2_g, ln2_b):
    h_lat, h_ctx = x, ctx
    for i in range(DEPTH):
        last = i == DEPTH - 1
        n_mod_ctx = (2 if last else 6) * D_MODEL
        m_lat = adaln(c, w_ada[i], b_ada[i])[:, None, :]
        sh1, sc1, g1, sh2, sc2, g2 = jnp.split(m_lat, 6, axis=-1)
        m_ctx = adaln(c_ctx, w_ada[i][:, :n_mod_ctx], b_ada[i][:n_mod_ctx])

        p_lat = modulate(h_lat, sh1, sc1) @ w_in[i]
        w_in_ctx = w_in[i][:, :COL_Z] if last else w_in[i]
        p_ctx = modulate(h_ctx, m_ctx[:D_MODEL], m_ctx[D_MODEL:2 * D_MODEL]) @ w_in_ctx
        y5_l, y5_c = s5_branch(p_lat[..., :S5_WIDTH], p_ctx[..., :S5_WIDTH], s5_lam_re[i], s5_lam_im[i],
                               s5_log_step[i], s5_b_re[i], s5_b_im[i], s5_c_re[i], s5_c_im[i], s5_d[i],
                               s5_w_glu[i], not last)
        z_ctx = None if last else p_ctx[..., COL_Z:COL_GATES]
        yd_l, yd_c = ssd_branch(p_lat[..., COL_XBC:COL_DT], p_lat[..., COL_DT:COL_Z], p_lat[..., COL_Z:COL_GATES],
                                p_ctx[..., COL_XBC:COL_DT], p_ctx[..., COL_DT:COL_Z], z_ctx,
                                ssd_conv_w[i], ssd_conv_b[i], ssd_dt_bias[i], ssd_a_log[i], ssd_d[i],
                                ssd_norm_w[i])
        out_l = merge_branches(y5_l, yd_l, p_lat[..., COL_GATES:], s5_w_proj[i], ssd_w_proj[i], w_out[i])
        h_lat = layer_norm(DEEPNORM_ALPHA * h_lat + g1 * out_l, ln1_g[i], ln1_b[i])

        f_l = conv_ffn(modulate(h_lat, sh2, sc2), w_up[i], ffn_conv_w[i], ffn_conv_b[i], w_down[i], True)
        h_lat = layer_norm(DEEPNORM_ALPHA * h_lat + g2 * f_l, ln2_g[i], ln2_b[i])

        if not last:
            c_g1, c_sh2, c_sc2, c_g2 = (m_ctx[2 * D_MODEL:3 * D_MODEL], m_ctx[3 * D_MODEL:4 * D_MODEL],
                                        m_ctx[4 * D_MODEL:5 * D_MODEL], m_ctx[5 * D_MODEL:])
            out_c = merge_branches(y5_c, yd_c, p_ctx[..., COL_GATES:], s5_w_proj[i], ssd_w_proj[i], w_out[i])
            h_ctx = layer_norm(DEEPNORM_ALPHA * h_ctx + c_g1 * out_c, ln1_g[i], ln1_b[i])
            f_c = conv_ffn(modulate(h_ctx, c_sh2, c_sc2), w_up[i], ffn_conv_w[i], ffn_conv_b[i], w_down[i], False)
            h_ctx = layer_norm(DEEPNORM_ALPHA * h_ctx + c_g2 * f_c, ln2_g[i], ln2_b[i])
    return h_lat
```

```python
import functools

import jax
import jax.numpy as jnp
from jax import lax
from jax.experimental import pallas as pl
from jax.experimental.pallas import tpu as pltpu

F32 = jnp.float32
BF16 = jnp.bfloat16
HIGHEST = lax.Precision.HIGHEST

D_MODEL = 2048
GRID_W = 64
S5_WIDTH = 1024
S5_GROUP = 16
S5_GROUPS = S5_WIDTH // S5_GROUP
S5_STATE = 64
S5_CHUNK = 16
S5_ROW = S5_CHUNK * S5_GROUP
SSD_WIDTH = 2048
SSD_HEAD_DIM = 64
SSD_HEADS = SSD_WIDTH // SSD_HEAD_DIM
SSD_GROUPS = 4
SSD_STATE = 128
SSD_CHUNK = 128
SSD_GW = SSD_WIDTH // SSD_GROUPS
SSD_CONV_CH = SSD_WIDTH + 2 * SSD_GROUPS * SSD_STATE
D_FF = 5632
COL_XBC = S5_WIDTH
COL_DT = COL_XBC + SSD_CONV_CH
COL_Z = COL_DT + 2 * SSD_HEADS
COL_GATES = COL_Z + SSD_WIDTH
LN_EPS = 1e-5
RMS_EPS = 1e-5

PM_XBC = 0
PM_U = SSD_CONV_CH
PM_Z = PM_U + S5_WIDTH
PM_G5 = PM_Z + SSD_WIDTH
PM_GD = PM_G5 + D_MODEL
PM_COLS = PM_GD + D_MODEL
DT_PAD = 128

VMEM_LIMIT = 56 * 1024 * 1024


def _cparams(*sem):
    return pltpu.CompilerParams(dimension_semantics=sem, vmem_limit_bytes=VMEM_LIMIT)


def _sigmoid(x):
    return 1.0 / (1.0 + jnp.exp(-x))


def _silu(x):
    return x * _sigmoid(x)


def _gelu_tanh(x):
    return 0.5 * x * (1.0 + jnp.tanh(0.7978845608028654 * (x + 0.044715 * x * x * x)))


def _softplus(x):
    return jnp.maximum(x, 0.0) + jnp.log1p(jnp.exp(-jnp.abs(x)))


def _layer_norm(t, gam, bet):
    mu = jnp.mean(t, axis=-1, keepdims=True)
    d = t - mu
    var = jnp.mean(d * d, axis=-1, keepdims=True)
    return d * lax.rsqrt(var + LN_EPS) * gam + bet


def _adaln_kernel(c_ref, w_ref, b_ref, o_ref):
    s = _silu(c_ref[...]).astype(BF16)
    o_ref[0] = jnp.dot(s, w_ref[0].astype(BF16), preferred_element_type=F32) + b_ref[0]


def _adaln(cc, w_ada, b_ada):
    depth, d, n6 = w_ada.shape
    tn = 1024
    return pl.pallas_call(
        _adaln_kernel,
        grid=(depth, n6 // tn),
        in_specs=[pl.BlockSpec((8, d), lambda l, j: (0, 0)),
                  pl.BlockSpec((1, d, tn), lambda l, j: (l, 0, j)),
                  pl.BlockSpec((1, 1, tn), lambda l, j: (l, 0, j))],
        out_specs=pl.BlockSpec((1, 8, tn), lambda l, j: (l, 0, j)),
        out_shape=jax.ShapeDtypeStruct((depth, 8, n6), F32),
        compiler_params=_cparams("parallel", "parallel"),
        name="adaln",
    )(cc, w_ada, b_ada.reshape(depth, 1, n6))


def _modulated(h_ref, m_ref, row, sh_off):
    d = h_ref.shape[1]
    sh = m_ref[row:row + 1, sh_off:sh_off + d]
    sc = m_ref[row:row + 1, sh_off + d:sh_off + 2 * d]
    return (h_ref[...] * (1.0 + sc) + sh).astype(BF16)


def _modmm_kernel(h_ref, m_ref, w_ref, o_ref, xs_ref, *, row, sh_off):
    @pl.when(pl.program_id(1) == 0)
    def _():
        xs_ref[...] = _modulated(h_ref, m_ref, row, sh_off)

    o_ref[...] = jnp.dot(xs_ref[...], w_ref[...], preferred_element_type=F32).astype(o_ref.dtype)


def _modmm_dt_kernel(h_ref, m_ref, w_ref, wdt_ref, o_ref, dt_ref, xs_ref, *, row, sh_off):
    @pl.when(pl.program_id(1) == 0)
    def _():
        xs = _modulated(h_ref, m_ref, row, sh_off)
        xs_ref[...] = xs
        dt_ref[...] = jnp.dot(xs, wdt_ref[...], preferred_element_type=F32)

    o_ref[...] = jnp.dot(xs_ref[...], w_ref[...], preferred_element_type=F32).astype(o_ref.dtype)


def _modmm(h, m, w, *, row, sh_off, wdt=None):
    n, d = h.shape
    ncols = w.shape[1]
    tm = min(1024, n)
    tn = 1024
    grid = (n // tm, ncols // tn)
    h_spec = pl.BlockSpec((tm, d), lambda i, j: (i, 0))
    m_spec = pl.BlockSpec(m.shape, lambda i, j: (0, 0))
    w_spec = pl.BlockSpec((d, tn), lambda i, j: (0, j))
    o_spec = pl.BlockSpec((tm, tn), lambda i, j: (i, j))
    scratch = [pltpu.VMEM((tm, d), BF16)]
    if wdt is None:
        return pl.pallas_call(
            functools.partial(_modmm_kernel, row=row, sh_off=sh_off),
            grid=grid, in_specs=[h_spec, m_spec, w_spec], out_specs=o_spec,
            out_shape=jax.ShapeDtypeStruct((n, ncols), BF16),
            scratch_shapes=scratch, compiler_params=_cparams("parallel", "arbitrary"),
            name="modmm",
        )(h, m, w)
    return pl.pallas_call(
        functools.partial(_modmm_dt_kernel, row=row, sh_off=sh_off),
        grid=grid,
        in_specs=[h_spec, m_spec, w_spec, pl.BlockSpec((d, DT_PAD), lambda i, j: (0, 0))],
        out_specs=[o_spec, pl.BlockSpec((tm, DT_PAD), lambda i, j: (i, 0))],
        out_shape=[jax.ShapeDtypeStruct((n, ncols), BF16), jax.ShapeDtypeStruct((n, DT_PAD), F32)],
        scratch_shapes=scratch, compiler_params=_cparams("parallel", "arbitrary"),
        name="modmm_dt",
    )(h, m, w, wdt)


def _s5_kernel(x_ref, wv_ref, wy_ref, q_ref, s0_ref, y_ref, fin_ref, v_ref, s_ref, *, gb, ls):
    c_rows = ls * 8
    lane = lax.broadcasted_iota(jnp.int32, (8, 128), 1)
    fwd = lane < 64
    row = lax.broadcasted_iota(jnp.int32, (8, 128), 0)

    for g in range(gb):
        v = jnp.dot(x_ref[g], wv_ref[g], preferred_element_type=F32)
        v_ref[g] = v.reshape(ls, 8, S5_ROW)

    qr = [q_ref[g, 0:1, :] for g in range(gb)]
    qi = [q_ref[g, 1:2, :] for g in range(gb)]

    def scan_body(i, carry):
        out = []
        for g in range(gb):
            cr, ci = carry[2 * g], carry[2 * g + 1]
            a = v_ref[g, i]
            b = v_ref[g, ls - 1 - i]
            vr = jnp.where(fwd, a[:, 0:128], b[:, 0:128])
            vi = jnp.where(fwd, a[:, 128:256], b[:, 128:256])
            s_ref[g, i, :, 0:64] = cr[:, 0:64]
            s_ref[g, i, :, 128:192] = ci[:, 0:64]
            s_ref[g, ls - 1 - i, :, 64:128] = cr[:, 64:128]
            s_ref[g, ls - 1 - i, :, 192:256] = ci[:, 64:128]
            out.append(qr[g] * cr - qi[g] * ci + vr)
            out.append(qr[g] * ci + qi[g] * cr + vi)
        return tuple(out)

    zero = jnp.zeros((8, 128), F32)
    tot = lax.fori_loop(0, ls, scan_body, tuple(zero for _ in range(2 * gb)))

    cins = []
    for g in range(gb):
        tr, ti = tot[2 * g], tot[2 * g + 1]
        lr, li = q_ref[g, 2:3, :], q_ref[g, 3:4, :]
        s0r, s0i = s0_ref[g, 0:1, :], s0_ref[g, 1:2, :]
        cf = [(s0r, s0i)]
        for s in range(1, 8):
            pr, pi = cf[-1]
            cf.append((lr * pr - li * pi + tr[s - 1:s, :], lr * pi + li * pr + ti[s - 1:s, :]))
        cb = [(s0r, s0i)]
        for s in range(6, -1, -1):
            pr, pi = cb[-1]
            cb.append((lr * pr - li * pi + tr[s + 1:s + 2, :], lr * pi + li * pr + ti[s + 1:s + 2, :]))
        cb = cb[::-1]
        cin_r, cin_i = zero, zero
        for s in range(8):
            sel = row == s
            cin_r = jnp.where(sel, jnp.where(fwd, cf[s][0], cb[s][0]), cin_r)
            cin_i = jnp.where(sel, jnp.where(fwd, cf[s][1], cb[s][1]), cin_i)
        cins += [cin_r, cin_i]
        ff_r = lr * cf[7][0] - li * cf[7][1] + tr[7:8, :]
        ff_i = lr * cf[7][1] + li * cf[7][0] + ti[7:8, :]
        fb_r = lr * cb[0][0] - li * cb[0][1] + tr[0:1, :]
        fb_i = lr * cb[0][1] + li * cb[0][0] + ti[0:1, :]
        fin_r = jnp.where(fwd[0:1], ff_r, fb_r)
        fin_i = jnp.where(fwd[0:1], ff_i, fb_i)
        fin_ref[g, 0:1, :] = fin_r
        fin_ref[g, 1:2, :] = fin_i
        fin_ref[g, 2:8, :] = jnp.zeros((6, 128), F32)

    def fix_body(i, carry):
        out = []
        for g in range(gb):
            dr, di = carry[2 * g], carry[2 * g + 1]
            add_f = jnp.concatenate([jnp.where(fwd, dr, 0.0), jnp.where(fwd, di, 0.0)], axis=1)
            s_ref[g, i] = s_ref[g, i] + add_f
            add_b = jnp.concatenate([jnp.where(fwd, 0.0, dr), jnp.where(fwd, 0.0, di)], axis=1)
            s_ref[g, ls - 1 - i] = s_ref[g, ls - 1 - i] + add_b
            out.append(qr[g] * dr - qi[g] * di)
            out.append(qr[g] * di + qi[g] * dr)
        return tuple(out)

    lax.fori_loop(0, ls, fix_body, tuple(cins))

    for g in range(gb):
        sin = s_ref[g].reshape(c_rows, S5_ROW).astype(BF16)
        y = jnp.dot(x_ref[g], wy_ref[g, 0:S5_ROW, :], preferred_element_type=F32)
        y = y + jnp.dot(sin, wy_ref[g, S5_ROW:2 * S5_ROW, :], preferred_element_type=F32)
        y_ref[g] = y.astype(BF16)


def _s5_scan(xg, wv, wy, q, s0):
    g, c_rows, _ = xg.shape
    ls = c_rows // 8
    gb = 4
    blk = lambda *shape: pl.BlockSpec((gb,) + shape, lambda i: (i,) + (0,) * len(shape))
    return pl.pallas_call(
        functools.partial(_s5_kernel, gb=gb, ls=ls),
        grid=(g // gb,),
        in_specs=[blk(c_rows, S5_ROW), blk(S5_ROW, S5_ROW), blk(2 * S5_ROW, S5_ROW), blk(8, 128), blk(8, 128)],
        out_specs=[blk(c_rows, S5_ROW), blk(8, 128)],
        out_shape=[jax.ShapeDtypeStruct((g, c_rows, S5_ROW), BF16), jax.ShapeDtypeStruct((g, 8, 128), F32)],
        scratch_shapes=[pltpu.VMEM((gb, ls, 8, S5_ROW), F32), pltpu.VMEM((gb, ls, 8, S5_ROW), F32)],
        compiler_params=_cparams("parallel"),
        name="s5_scan",
    )(xg, wv, wy, q, s0)


def _s5_operators(lam_re, lam_im, log_step, b_re, b_im, c_re, c_im, d, seg_lens):
    t = S5_CHUNK
    depth = lam_re.shape[0]
    lre = jnp.minimum(lam_re.astype(F32), -1e-4)
    lim = lam_im.astype(F32)
    step = jnp.exp(log_step.astype(F32))[..., None]

    def apow(k):
        mag = jnp.exp(k * lre * step)
        return mag * jnp.cos(k * lim * step), mag * jnp.sin(k * lim * step)

    ab_re, ab_im = apow(1.0)
    den = jnp.square(lre) + jnp.square(lim)
    q_re = ((ab_re - 1.0) * lre + ab_im * lim) / den
    q_im = (ab_im * lre - (ab_re - 1.0) * lim) / den
    bre = b_re.astype(F32)[:, None]
    bim = b_im.astype(F32)[:, None]
    bb_re = q_re[..., None] * bre - q_im[..., None] * bim
    bb_im = q_re[..., None] * bim + q_im[..., None] * bre

    ks = jnp.arange(t + 1, dtype=F32)[:, None, None, None, None]
    pw_re, pw_im = apow(ks)
    pw_re = jnp.moveaxis(pw_re, 0, 3)
    pw_im = jnp.moveaxis(pw_im, 0, 3)
    w_re = pw_re[..., None] * bb_re[:, :, :, None] - pw_im[..., None] * bb_im[:, :, :, None]
    w_im = pw_re[..., None] * bb_im[:, :, :, None] + pw_im[..., None] * bb_re[:, :, :, None]
    cre = c_re.astype(F32)
    cim = c_im.astype(F32)
    kk = (jnp.einsum('ldgjp,ldgkpi->ldgkji', cre, w_re, precision=HIGHEST)
          - jnp.einsum('ldgjp,ldgkpi->ldgkji', cim, w_im, precision=HIGHEST))
    kk = kk[:, :, :, :t]
    ti = jnp.arange(t)
    lag = ti[None, :] - ti[:, None]
    kf = jnp.take(kk[:, 0], jnp.clip(lag, 0, t - 1), axis=2)
    kb = jnp.take(kk[:, 1], jnp.clip(-lag, 0, t - 1), axis=2)
    mf = (lag >= 0)[None, None, :, :, None, None]
    mb = (lag <= 0)[None, None, :, :, None, None]
    dd = d.astype(F32).reshape(depth, S5_GROUPS, S5_GROUP)
    skip = (dd[:, :, None, None, :, None] * jnp.eye(S5_GROUP, dtype=F32)[None, None, None, None]
            * (lag == 0)[None, None, :, :, None, None])
    toe = jnp.where(mf, kf, 0.0) + jnp.where(mb, kb, 0.0) + skip
    toe = toe.transpose(0, 1, 2, 5, 3, 4).reshape(depth, S5_GROUPS, S5_ROW, S5_ROW)

    g_re = cre[:, :, :, None] * pw_re[:, :, :, :, None] - cim[:, :, :, None] * pw_im[:, :, :, :, None]
    g_im = cre[:, :, :, None] * pw_im[:, :, :, :, None] + cim[:, :, :, None] * pw_re[:, :, :, :, None]
    ef = ti + 1
    eb = t - ti

    def rd(arr, dr, e):
        return arr[:, dr][:, :, e].transpose(0, 1, 4, 2, 3).reshape(depth, S5_GROUPS, S5_STATE, S5_ROW)

    rd_rows = jnp.concatenate([rd(g_re, 0, ef), rd(g_re, 1, eb), -rd(g_im, 0, ef), -rd(g_im, 1, eb)], axis=2)
    wy = jnp.concatenate([toe, rd_rows], axis=2).astype(BF16)

    def sm(arr, dr, e):
        return arr[:, dr][:, :, e].transpose(0, 1, 2, 4, 3).reshape(depth, S5_GROUPS, S5_ROW, S5_STATE)

    e_f = t - 1 - ti
    wv = jnp.concatenate([sm(w_re, 0, e_f), sm(w_re, 1, ti), sm(w_im, 0, e_f), sm(w_im, 1, ti)], axis=3).astype(BF16)

    def lanes(re_im):
        return jnp.concatenate([re_im[:, 0], re_im[:, 1]], axis=-1)[:, :, None, :]

    c_r, c_i = apow(float(t))
    qs = []
    for ls in seg_lens:
        s_r, s_i = apow(float(t * ls))
        qs.append(jnp.concatenate([lanes(c_r), lanes(c_i), lanes(s_r), lanes(s_i),
                                   jnp.zeros((depth, S5_GROUPS, 4, 128), F32)], axis=2))
    return wv, wy, qs


def _s5_to_chunks(u):
    n = u.shape[0]
    ls = n // (8 * S5_CHUNK)
    x = u.reshape(8, ls, S5_CHUNK, S5_GROUPS, S5_GROUP).transpose(3, 1, 0, 2, 4)
    return x.reshape(S5_GROUPS, 8 * ls, S5_ROW)


def _s5_from_chunks(y):
    g, c_rows, _ = y.shape
    ls = c_rows // 8
    x = y.reshape(g, ls, 8, S5_CHUNK, S5_GROUP).transpose(2, 1, 3, 0, 4)
    return x.reshape(c_rows * S5_CHUNK, S5_WIDTH)


def _glu_kernel(y_ref, wa_ref, wb_ref, o_ref):
    gl = _gelu_tanh(y_ref[...].astype(F32)).astype(BF16)
    a = jnp.dot(gl, wa_ref[...], preferred_element_type=F32)
    b = jnp.dot(gl, wb_ref[...], preferred_element_type=F32)
    o_ref[...] = (a * _sigmoid(b)).astype(BF16)


def _s5_glu(y, w_glu):
    n, k = y.shape
    tm = min(1024, n)
    return pl.pallas_call(
        _glu_kernel,
        grid=(n // tm,),
        in_specs=[pl.BlockSpec((tm, k), lambda i: (i, 0)),
                  pl.BlockSpec((k, S5_WIDTH), lambda i: (0, 0)),
                  pl.BlockSpec((k, S5_WIDTH), lambda i: (0, 1))],
        out_specs=pl.BlockSpec((tm, S5_WIDTH), lambda i: (i, 0)),
        out_shape=jax.ShapeDtypeStruct((n, S5_WIDTH), BF16),
        compiler_params=_cparams("parallel"),
        name="s5_glu",
    )(y, w_glu, w_glu)


def _ssdconv_kernel(x_ref, xp_ref, xn_ref, w_ref, b_ref, o_ref, *, nt):
    i = pl.program_id(0)
    x = x_ref[...].astype(F32)
    tr = x.shape[0]
    prev = jnp.where(i > 0, xp_ref[...].astype(F32)[15:16, :], 0.0)
    nxt = jnp.where(i < nt - 1, xn_ref[...].astype(F32)[0:1, :], 0.0)
    ri = lax.broadcasted_iota(jnp.int32, x.shape, 0)
    x_m1 = jnp.where(ri == 0, prev, pltpu.roll(x, 1, 0))
    x_p1 = jnp.where(ri == tr - 1, nxt, pltpu.roll(x, tr - 1, 0))
    y = w_ref[0:1, :] * x_m1 + w_ref[1:2, :] * x + w_ref[2:3, :] * x_p1 + b_ref[...]
    o_ref[...] = _silu(y).astype(BF16)


def _ssd_conv(pm, conv_w, conv_b):
    n = pm.shape[0]
    tr = min(512, n)
    tc = 1024
    nt = n // tr
    hb = tr // 16
    return pl.pallas_call(
        functools.partial(_ssdconv_kernel, nt=nt),
        grid=(nt, SSD_CONV_CH // tc),
        in_specs=[pl.BlockSpec((tr, tc), lambda i, j: (i, j)),
                  pl.BlockSpec((16, tc), lambda i, j: (jnp.maximum(i * hb - 1, 0), j)),
                  pl.BlockSpec((16, tc), lambda i, j: (jnp.minimum((i + 1) * hb, n // 16 - 1), j)),
                  pl.BlockSpec((3, tc), lambda i, j: (0, j)),
                  pl.BlockSpec((1, tc), lambda i, j: (0, j))],
        out_specs=pl.BlockSpec((tr, tc), lambda i, j: (i, j)),
        out_shape=jax.ShapeDtypeStruct((n, SSD_CONV_CH), BF16),
        compiler_params=_cparams("parallel", "parallel"),
        name="ssd_conv",
    )(pm, pm, pm, conv_w, conv_b)


def _head_expand(v, first_row):
    rows = lax.broadcasted_iota(jnp.int32, (2 * SSD_HEADS, SSD_WIDTH), 0)
    cols = lax.broadcasted_iota(jnp.int32, (2 * SSD_HEADS, SSD_WIDTH), 1)
    onehot = jnp.where((cols >> 6) + first_row == rows, 1.0, 0.0).astype(BF16)
    hi = v.astype(BF16)
    lo = (v - hi.astype(F32)).astype(BF16)
    return (jnp.dot(hi, onehot, preferred_element_type=F32) + jnp.dot(lo, onehot, preferred_element_type=F32))


def _tri_masks(t):
    r = lax.broadcasted_iota(jnp.int32, (t, t), 0)
    c = lax.broadcasted_iota(jnp.int32, (t, t), 1)
    return r, c


def _ssd_fwd_kernel(xc_ref, dt_ref, dtt_ref, bias_ref, biast_ref, a_ref, at_ref, s0_ref,
                    y_ref, sfin_ref, s_ref, *, nc):
    t = SSD_CHUNK
    c = pl.program_id(0)

    @pl.when(c == 0)
    def _():
        s_ref[...] = s0_ref[...]

    dt = _softplus(dt_ref[:, 0:2 * SSD_HEADS] + bias_ref[...])
    dtt = _softplus(dtt_ref[...] + biast_ref[...])
    la = dt * a_ref[...]
    lat = dtt * at_ref[...]
    r, cc = _tri_masks(t)
    lo_incl = jnp.where(r >= cc, 1.0, 0.0)
    up_incl = jnp.where(r <= cc, 1.0, 0.0)
    cum_lo = jnp.dot(lo_incl, la, preferred_element_type=F32, precision=HIGHEST)
    cum_up = jnp.dot(up_incl, la, preferred_element_type=F32, precision=HIGHEST)
    cumt_f = jnp.dot(lat, up_incl, preferred_element_type=F32, precision=HIGHEST)
    cumt_b = jnp.dot(lat, lo_incl, preferred_element_type=F32, precision=HIGHEST)
    lo = r > cc
    up = r < cc

    end_f = cum_lo[t - 1:t, :]
    exp_f = _head_expand(jnp.exp(cum_lo), 0)
    w_f = _head_expand(jnp.exp(end_f - cum_lo) * dt, 0)
    lane = lax.broadcasted_iota(jnp.int32, (t, 128), 1)

    for g in range(SSD_GROUPS):
        gs = slice(g * SSD_GW, (g + 1) * SSD_GW)
        bg = xc_ref[:, SSD_WIDTH + g * SSD_STATE:SSD_WIDTH + (g + 1) * SSD_STATE]
        cg = xc_ref[:, SSD_WIDTH + (SSD_GROUPS + g) * SSD_STATE:SSD_WIDTH + (SSD_GROUPS + g + 1) * SSD_STATE]
        cb = lax.dot_general(cg, bg, (((1,), (1,)), ((), ())), preferred_element_type=F32)
        bt = bg.astype(F32).T.astype(BF16)
        xg = xc_ref[:, gs]
        sg = s_ref[g]
        y_int = exp_f[:, gs] * jnp.dot(cg, sg.astype(BF16), preferred_element_type=F32)
        xw = (xg.astype(F32) * w_f[:, gs]).astype(BF16)
        s_ref[g] = sg * exp_f[t - 1:t, gs] + jnp.dot(bt, xw, preferred_element_type=F32)
        for p in range(SSD_GW // 128):
            xp = xg[:, p * 128:(p + 1) * 128]
            acc = y_int[:, p * 128:(p + 1) * 128]
            for q in range(2):
                hh = g * (SSD_HEADS // SSD_GROUPS) + 2 * p + q
                hb = SSD_HEADS + hh
                arg = jnp.where(lo, cum_lo[:, hh:hh + 1] - cumt_f[hh:hh + 1, :],
                                jnp.where(up, cum_up[:, hb:hb + 1] - cumt_b[hb:hb + 1, :], 0.0))
                dsel = jnp.where(lo, dtt[hh:hh + 1, :],
                                 jnp.where(up, dtt[hb:hb + 1, :], dtt[hh:hh + 1, :] + dtt[hb:hb + 1, :]))
                sc = (cb * jnp.exp(arg) * dsel).astype(BF16)
                xh = jnp.where((lane < 64) if q == 0 else (lane >= 64), xp, jnp.zeros_like(xp))
                acc = acc + jnp.dot(sc, xh, preferred_element_type=F32)
            y_ref[:, g * SSD_GW + p * 128:g * SSD_GW + (p + 1) * 128] = acc

    @pl.when(c == nc - 1)
    def _():
        sfin_ref[...] = s_ref[...]


def _ssd_bwd_kernel(xc_ref, dt_ref, z_ref, yp_ref, bias_ref, a_ref, dvec_ref, nw_ref, s0_ref,
                    o_ref, sfin_ref, s_ref, *, nc):
    t = SSD_CHUNK
    c = pl.program_id(0)

    @pl.when(c == 0)
    def _():
        s_ref[...] = s0_ref[...]

    dt = _softplus(dt_ref[:, 0:2 * SSD_HEADS] + bias_ref[...])
    la = dt * a_ref[...]
    r, cc = _tri_masks(t)
    up_incl = jnp.where(r <= cc, 1.0, 0.0)
    cum_up = jnp.dot(up_incl, la, preferred_element_type=F32, precision=HIGHEST)
    exp_b = _head_expand(jnp.exp(cum_up), SSD_HEADS)
    w_b = _head_expand(jnp.exp(cum_up[0:1, :] - cum_up) * dt, SSD_HEADS)

    for g in range(SSD_GROUPS):
        gs = slice(g * SSD_GW, (g + 1) * SSD_GW)
        bg = xc_ref[:, SSD_WIDTH + g * SSD_STATE:SSD_WIDTH + (g + 1) * SSD_STATE]
        cg = xc_ref[:, SSD_WIDTH + (SSD_GROUPS + g) * SSD_STATE:SSD_WIDTH + (SSD_GROUPS + g + 1) * SSD_STATE]
        bt = bg.astype(F32).T.astype(BF16)
        xg = xc_ref[:, gs].astype(F32)
        sg = s_ref[g]
        y_int = exp_b[:, gs] * jnp.dot(cg, sg.astype(BF16), preferred_element_type=F32)
        xw = (xg * w_b[:, gs]).astype(BF16)
        s_ref[g] = sg * exp_b[0:1, gs] + jnp.dot(bt, xw, preferred_element_type=F32)
        y = yp_ref[:, gs] + y_int + xg * dvec_ref[:, gs]
        hcur = y * _silu(z_ref[:, gs].astype(F32))
        ms = jnp.mean(hcur * hcur, axis=-1, keepdims=True)
        o_ref[:, gs] = (hcur * lax.rsqrt(ms + RMS_EPS) * nw_ref[:, gs]).astype(BF16)

    @pl.when(c == nc - 1)
    def _():
        sfin_ref[...] = s_ref[...]


def _ssd_scan(xc, dt_raw, dt_t, pm, prm, s0_f, s0_b):
    n = xc.shape[0]
    t = SSD_CHUNK
    nc = n // t
    full = lambda a: pl.BlockSpec(a.shape, lambda c: (0,) * a.ndim)
    s_shape = (SSD_GROUPS, SSD_STATE, SSD_GW)
    s_spec = pl.BlockSpec(s_shape, lambda c: (0, 0, 0))
    y_part, sfin_f = pl.pallas_call(
        functools.partial(_ssd_fwd_kernel, nc=nc),
        grid=(nc,),
        in_specs=[pl.BlockSpec((t, SSD_CONV_CH), lambda c: (c, 0)),
                  pl.BlockSpec((t, DT_PAD), lambda c: (c, 0)),
                  pl.BlockSpec((2 * SSD_HEADS, t), lambda c: (0, c)),
                  full(prm['bias']), full(prm['bias_t']), full(prm['a']), full(prm['a_t']), s_spec],
        out_specs=[pl.BlockSpec((t, SSD_WIDTH), lambda c: (c, 0)), s_spec],
        out_shape=[jax.ShapeDtypeStruct((n, SSD_WIDTH), F32), jax.ShapeDtypeStruct(s_shape, F32)],
        scratch_shapes=[pltpu.VMEM(s_shape, F32)],
        compiler_params=_cparams("arbitrary"),
        name="ssd_fwd",
    )(xc, dt_raw, dt_t, prm['bias'], prm['bias_t'], prm['a'], prm['a_t'], s0_f)
    rev = lambda c: (nc - 1 - c, 0)
    z_blk = PM_Z // SSD_WIDTH
    yd, sfin_b = pl.pallas_call(
        functools.partial(_ssd_bwd_kernel, nc=nc),
        grid=(nc,),
        in_specs=[pl.BlockSpec((t, SSD_CONV_CH), rev),
                  pl.BlockSpec((t, DT_PAD), rev),
                  pl.BlockSpec((t, SSD_WIDTH), lambda c: (nc - 1 - c, z_blk)),
                  pl.BlockSpec((t, SSD_WIDTH), rev),
                  full(prm['bias']), full(prm['a']), full(prm['dvec']), full(prm['norm_w']), s_spec],
        out_specs=[pl.BlockSpec((t, SSD_WIDTH), rev), s_spec],
        out_shape=[jax.ShapeDtypeStruct((n, SSD_WIDTH), BF16), jax.ShapeDtypeStruct(s_shape, F32)],
        scratch_shapes=[pltpu.VMEM(s_shape, F32)],
        compiler_params=_cparams("arbitrary"),
        name="ssd_bwd",
    )(xc, dt_raw, pm, y_part, prm['bias'], prm['a'], prm['dvec'], prm['norm_w'], s0_b)
    return yd, sfin_f, sfin_b


def _merge_kernel(y5_ref, yd_ref, w5_ref, wd_ref, g5_ref, gd_ref, o_ref):
    a = jnp.dot(y5_ref[...], w5_ref[...], preferred_element_type=F32)
    b = jnp.dot(yd_ref[...], wd_ref[...], preferred_element_type=F32)
    o = _sigmoid(g5_ref[...].astype(F32)) * a + _sigmoid(gd_ref[...].astype(F32)) * b
    o_ref[...] = o.astype(BF16)


def _merge(y5, yd, w5, wd, pm):
    n = y5.shape[0]
    tm = min(1024, n)
    tn = 1024
    g5_blk, gd_blk = PM_G5 // tn, PM_GD // tn
    return pl.pallas_call(
        _merge_kernel,
        grid=(n // tm, D_MODEL // tn),
        in_specs=[pl.BlockSpec((tm, S5_WIDTH), lambda i, j: (i, 0)),
                  pl.BlockSpec((tm, SSD_WIDTH), lambda i, j: (i, 0)),
                  pl.BlockSpec((S5_WIDTH, tn), lambda i, j: (0, j)),
                  pl.BlockSpec((SSD_WIDTH, tn), lambda i, j: (0, j)),
                  pl.BlockSpec((tm, tn), lambda i, j: (i, g5_blk + j)),
                  pl.BlockSpec((tm, tn), lambda i, j: (i, gd_blk + j))],
        out_specs=pl.BlockSpec((tm, tn), lambda i, j: (i, j)),
        out_shape=jax.ShapeDtypeStruct((n, D_MODEL), BF16),
        compiler_params=_cparams("parallel", "parallel"),
        name="merge",
    )(y5, yd, w5, wd, pm, pm)


def _proj_ln_kernel(x_ref, w_ref, h_ref, m_ref, lg_ref, lb_ref, o_ref, acc_ref, *, row, g_off, nk, alpha):
    k = pl.program_id(1)
    part = jnp.dot(x_ref[...], w_ref[...], preferred_element_type=F32)

    @pl.when(k == 0)
    def _():
        acc_ref[...] = part

    @pl.when(k > 0)
    def _():
        acc_ref[...] += part

    @pl.when(k == nk - 1)
    def _():
        d = h_ref.shape[1]
        gate = m_ref[row:row + 1, g_off:g_off + d]
        o_ref[...] = _layer_norm(alpha * h_ref[...] + gate * acc_ref[...], lg_ref[...], lb_ref[...])


def _proj_ln(x, w, h, m, ln_g, ln_b, *, row, g_off, tk, alpha):
    n, kdim = x.shape
    d = h.shape[1]
    tm = min(512, n)
    nk = kdim // tk
    return pl.pallas_call(
        functools.partial(_proj_ln_kernel, row=row, g_off=g_off, nk=nk, alpha=alpha),
        grid=(n // tm, nk),
        in_specs=[pl.BlockSpec((tm, tk), lambda i, k: (i, k)),
                  pl.BlockSpec((tk, d), lambda i, k: (k, 0)),
                  pl.BlockSpec((tm, d), lambda i, k: (i, 0)),
                  pl.BlockSpec(m.shape, lambda i, k: (0, 0)),
                  pl.BlockSpec((1, d), lambda i, k: (0, 0)),
                  pl.BlockSpec((1, d), lambda i, k: (0, 0))],
        out_specs=pl.BlockSpec((tm, d), lambda i, k: (i, 0)),
        out_shape=jax.ShapeDtypeStruct((n, d), F32),
        scratch_shapes=[pltpu.VMEM((tm, d), F32)],
        compiler_params=_cparams("parallel", "arbitrary"),
        name="proj_ln",
    )(x, w, h, m, ln_g, ln_b)


FFN_HALO = 128


def _ffnconv_kernel(g_ref, gp_ref, gn_ref, v_ref, w_ref, b_ref, o_ref, ext_ref, *, nt, grid_mode):
    i = pl.program_id(0)
    tr = g_ref.shape[0]
    hl = FFN_HALO
    ext_ref[0:hl, :] = jnp.where(i > 0, gp_ref[...].astype(F32), 0.0)
    ext_ref[hl:hl + tr, :] = g_ref[...].astype(F32)
    ext_ref[hl + tr:hl + tr + hl, :] = jnp.where(i < nt - 1, gn_ref[...].astype(F32), 0.0)

    def tap(dy, dx):
        wrow = (dy + 1) * 3 + (dx + 1)
        return w_ref[wrow:wrow + 1, :] * ext_ref[pl.ds(hl + GRID_W * dy + dx, tr), :]

    if grid_mode:
        col = lax.broadcasted_iota(jnp.int32, (tr, g_ref.shape[1]), 0) & (GRID_W - 1)
        acc_c = tap(-1, 0) + tap(0, 0) + tap(1, 0)
        acc_l = tap(-1, -1) + tap(0, -1) + tap(1, -1)
        acc_r = tap(-1, 1) + tap(0, 1) + tap(1, 1)
        y = acc_c + jnp.where(col != 0, acc_l, 0.0) + jnp.where(col != GRID_W - 1, acc_r, 0.0)
    else:
        y = tap(0, -1) + tap(0, 0) + tap(0, 1)
    y = y + b_ref[...]
    o_ref[...] = (_silu(y) * v_ref[...].astype(F32)).astype(BF16)


def _ffn_conv(up, conv_w9, conv_b, *, grid_mode):
    n = up.shape[0]
    tr = min(1024, n)
    tc = 512
    nt = n // tr
    hb = tr // FFN_HALO
    ncb = D_FF // tc
    return pl.pallas_call(
        functools.partial(_ffnconv_kernel, nt=nt, grid_mode=grid_mode),
        grid=(nt, ncb),
        in_specs=[pl.BlockSpec((tr, tc), lambda i, j: (i, j)),
                  pl.BlockSpec((FFN_HALO, tc), lambda i, j: (jnp.maximum(i * hb - 1, 0), j)),
                  pl.BlockSpec((FFN_HALO, tc), lambda i, j: (jnp.minimum((i + 1) * hb, n // FFN_HALO - 1), j)),
                  pl.BlockSpec((tr, tc), lambda i, j: (i, ncb + j)),
                  pl.BlockSpec((9, tc), lambda i, j: (0, j)),
                  pl.BlockSpec((1, tc), lambda i, j: (0, j))],
        out_specs=pl.BlockSpec((tr, tc), lambda i, j: (i, j)),
        out_shape=jax.ShapeDtypeStruct((n, D_FF), BF16),
        scratch_shapes=[pltpu.VMEM((tr + 2 * FFN_HALO, tc), F32)],
        compiler_params=_cparams("parallel", "parallel"),
        name="ffn_conv",
    )(up, up, up, up, conv_w9, conv_b)


def _mixer(h, m, row, lw, q, s5_init, ssd_init):
    pm, dt_raw = _modmm(h, m, lw['w_main'], row=row, sh_off=0, wdt=lw['w_dt'])
    xg = _s5_to_chunks(pm[:, PM_U:PM_U + S5_WIDTH])
    y5g, s5_fin = _s5_scan(xg, lw['s5_wv'], lw['s5_wy'], q, s5_init)
    y5 = _s5_glu(_s5_from_chunks(y5g), lw['s5_w_glu'])
    xc = _ssd_conv(pm, lw['ssd_conv_w'], lw['ssd_conv_b'])
    dt_t = dt_raw[:, 0:2 * SSD_HEADS].T
    yd, sf, sb = _ssd_scan(xc, dt_raw, dt_t, pm, lw['ssd'], ssd_init[0], ssd_init[1])
    return pm, y5, yd, s5_fin, (sf, sb)


def _rest_of_layer(h, m, row, lw, pm, y5, yd, alpha, *, grid_mode):
    d = D_MODEL
    merged = _merge(y5, yd, lw['s5_w_proj'], lw['ssd_w_proj'], pm)
    h1 = _proj_ln(merged, lw['w_out'], h, m, lw['ln1_g'], lw['ln1_b'], row=row, g_off=2 * d, tk=d, alpha=alpha)
    up = _modmm(h1, m, lw['w_up'], row=row, sh_off=3 * d)
    act = _ffn_conv(up, lw['ffn_conv_w'], lw['ffn_conv_b'], grid_mode=grid_mode)
    return _proj_ln(act, lw['w_down'], h1, m, lw['ln2_g'], lw['ln2_b'], row=row, g_off=5 * d, tk=D_FF // 4, alpha=alpha)


def kernel(x, c, ctx, c_ctx, w_ada, b_ada, w_in, s5_lam_re, s5_lam_im, s5_log_step, s5_b_re, s5_b_im, s5_c_re, s5_c_im, s5_d, s5_w_glu, s5_w_proj, ssd_conv_w, ssd_conv_b, ssd_dt_bias, ssd_a_log, ssd_d, ssd_norm_w, ssd_w_proj, w_out, ln1_g, ln1_b, w_up, ffn_conv_w, ffn_conv_b, w_down, ln2_g, ln2_b):
    depth = w_ada.shape[0]
    alpha = float((2 * depth) ** 0.25)
    n_lat, n_ctx = x.shape[1], ctx.shape[1]
    h_lat, h_ctx = x[0], ctx[0]

    cc = jnp.zeros((8, D_MODEL), F32).at[0].set(c[0]).at[1].set(c_ctx)
    mods = _adaln(cc, w_ada, b_ada)

    seg = (n_lat // (8 * S5_CHUNK), n_ctx // (8 * S5_CHUNK))
    s5_wv, s5_wy, (q_lat, q_ctx) = _s5_operators(s5_lam_re, s5_lam_im, s5_log_step, s5_b_re, s5_b_im,
                                                 s5_c_re, s5_c_im, s5_d, seg)

    w_main = jnp.concatenate([w_in[:, :, COL_XBC:COL_DT], w_in[:, :, :COL_XBC], w_in[:, :, COL_Z:]], axis=2).astype(BF16)
    w_dt = jnp.pad(w_in[:, :, COL_DT:COL_Z], ((0, 0), (0, 0), (0, DT_PAD - 2 * SSD_HEADS))).astype(BF16)
    a64 = (-jnp.exp(ssd_a_log.astype(F32))).reshape(depth, 1, 2 * SSD_HEADS)
    bias64 = ssd_dt_bias.astype(F32).reshape(depth, 1, 2 * SSD_HEADS)
    dvec = jnp.repeat(ssd_d.astype(F32), SSD_HEAD_DIM, axis=1).reshape(depth, 1, SSD_WIDTH)

    s5_zero = jnp.zeros((S5_GROUPS, 8, 128), F32)
    ssd_zero = jnp.zeros((SSD_GROUPS, SSD_STATE, SSD_GW), F32)

    for i in range(depth):
        last = i == depth - 1
        lw = {
            'w_main': w_main[i], 'w_dt': w_dt[i], 's5_wv': s5_wv[i], 's5_wy': s5_wy[i],
            's5_w_glu': s5_w_glu[i].astype(BF16), 's5_w_proj': s5_w_proj[i].astype(BF16),
            'ssd_conv_w': ssd_conv_w[i], 'ssd_conv_b': ssd_conv_b[i].reshape(1, SSD_CONV_CH),
            'ssd': {'bias': bias64[i], 'bias_t': bias64[i].reshape(2 * SSD_HEADS, 1),
                    'a': a64[i], 'a_t': a64[i].reshape(2 * SSD_HEADS, 1),
                    'dvec': dvec[i], 'norm_w': ssd_norm_w[i].astype(F32).reshape(1, SSD_WIDTH)},
            'ssd_w_proj': ssd_w_proj[i].astype(BF16), 'w_out': w_out[i].astype(BF16),
            'ln1_g': ln1_g[i].reshape(1, D_MODEL), 'ln1_b': ln1_b[i].reshape(1, D_MODEL),
            'w_up': w_up[i].astype(BF16), 'ffn_conv_w': ffn_conv_w[i].reshape(9, D_FF),
            'ffn_conv_b': ffn_conv_b[i].reshape(1, D_FF), 'w_down': w_down[i].astype(BF16),
            'ln2_g': ln2_g[i].reshape(1, D_MODEL), 'ln2_b': ln2_b[i].reshape(1, D_MODEL),
        }
        m = mods[i]
        pm_c, y5_c, yd_c, s5_fin, ssd_fin = _mixer(h_ctx, m, 1, lw, q_ctx[i], s5_zero, (ssd_zero, ssd_zero))
        pm_l, y5_l, yd_l, _, _ = _mixer(h_lat, m, 0, lw, q_lat[i], s5_fin, ssd_fin)
        h_lat = _rest_of_layer(h_lat, m, 0, lw, pm_l, y5_l, yd_l, alpha, grid_mode=True)
        if not last:
            h_ctx = _rest_of_layer(h_ctx, m, 1, lw, pm_c, y5_c, yd_c, alpha, grid_mode=False)
    return h_lat[None]
```

```python
import functools

import jax
import jax.numpy as jnp
from jax import lax
from jax.experimental import pallas as pl
from jax.experimental.pallas import tpu as pltpu

F32 = jnp.float32
BF16 = jnp.bfloat16
HIGHEST = lax.Precision.HIGHEST

D_MODEL = 2048
GRID_W = 64
S5_WIDTH = 1024
S5_GROUP = 16
S5_GROUPS = S5_WIDTH // S5_GROUP
S5_STATE = 64
S5_CHUNK = 16
S5_ROW = S5_CHUNK * S5_GROUP
S5_LB = 128 // S5_GROUP
S5_XW = S5_CHUNK * 128
S5_NLAG = 2 * S5_CHUNK - 1
SSD_WIDTH = 2048
SSD_HEAD_DIM = 64
SSD_HEADS = SSD_WIDTH // SSD_HEAD_DIM
SSD_GROUPS = 4
SSD_STATE = 128
SSD_CHUNK = 128
SSD_GW = SSD_WIDTH // SSD_GROUPS
SSD_CONV_CH = SSD_WIDTH + 2 * SSD_GROUPS * SSD_STATE
D_FF = 5632
COL_XBC = S5_WIDTH
COL_DT = COL_XBC + SSD_CONV_CH
COL_Z = COL_DT + 2 * SSD_HEADS
COL_GATES = COL_Z + SSD_WIDTH
LN_EPS = 1e-5
RMS_EPS = 1e-5

PM_XBC = 0
PM_U = SSD_CONV_CH
PM_Z = PM_U + S5_WIDTH
PM_G5 = PM_Z + SSD_WIDTH
PM_GD = PM_G5 + D_MODEL
PM_COLS = PM_GD + D_MODEL
DT_PAD = 128

VMEM_LIMIT = 56 * 1024 * 1024


def _cparams(*sem):
    return pltpu.CompilerParams(dimension_semantics=sem, vmem_limit_bytes=VMEM_LIMIT)


def _sigmoid(x):
    return 1.0 / (1.0 + jnp.exp(-x))


def _silu(x):
    return x * _sigmoid(x)


def _gelu_tanh(x):
    return 0.5 * x * (1.0 + jnp.tanh(0.7978845608028654 * (x + 0.044715 * x * x * x)))


def _softplus(x):
    return jnp.maximum(x, 0.0) + jnp.log1p(jnp.exp(-jnp.abs(x)))


def _layer_norm(t, gam, bet):
    mu = jnp.mean(t, axis=-1, keepdims=True)
    d = t - mu
    var = jnp.mean(d * d, axis=-1, keepdims=True)
    return d * lax.rsqrt(var + LN_EPS) * gam + bet


def _adaln_kernel(c_ref, w_ref, b_ref, o_ref):
    s = _silu(c_ref[...]).astype(BF16)
    o_ref[0] = jnp.dot(s, w_ref[0].astype(BF16), preferred_element_type=F32) + b_ref[0]


def _adaln(cc, w_ada, b_ada):
    depth, d, n6 = w_ada.shape
    tn = 1024
    return pl.pallas_call(
        _adaln_kernel,
        grid=(depth, n6 // tn),
        in_specs=[pl.BlockSpec((8, d), lambda l, j: (0, 0)),
                  pl.BlockSpec((1, d, tn), lambda l, j: (l, 0, j)),
                  pl.BlockSpec((1, 1, tn), lambda l, j: (l, 0, j))],
        out_specs=pl.BlockSpec((1, 8, tn), lambda l, j: (l, 0, j)),
        out_shape=jax.ShapeDtypeStruct((depth, 8, n6), F32),
        compiler_params=_cparams("parallel", "parallel"),
        name="adaln",
    )(cc, w_ada, b_ada.reshape(depth, 1, n6))


def _modulated(h_ref, m_ref, row, sh_off):
    d = h_ref.shape[1]
    sh = m_ref[row:row + 1, sh_off:sh_off + d]
    sc = m_ref[row:row + 1, sh_off + d:sh_off + 2 * d]
    return (h_ref[...] * (1.0 + sc) + sh).astype(BF16)


def _modmm_kernel(h_ref, m_ref, w_ref, o_ref, xs_ref, *, row, sh_off):
    @pl.when(pl.program_id(1) == 0)
    def _():
        xs_ref[...] = _modulated(h_ref, m_ref, row, sh_off)

    o_ref[...] = jnp.dot(xs_ref[...], w_ref[...], preferred_element_type=F32).astype(o_ref.dtype)


def _modmm_dt_kernel(h_ref, m_ref, w_ref, wdt_ref, o_ref, dt_ref, xs_ref, *, row, sh_off):
    @pl.when(pl.program_id(1) == 0)
    def _():
        xs = _modulated(h_ref, m_ref, row, sh_off)
        xs_ref[...] = xs
        dt_ref[...] = jnp.dot(xs, wdt_ref[...], preferred_element_type=F32)

    o_ref[...] = jnp.dot(xs_ref[...], w_ref[...], preferred_element_type=F32).astype(o_ref.dtype)


def _modmm(h, m, w, *, row, sh_off, wdt=None):
    n, d = h.shape
    ncols = w.shape[1]
    tm = min(1024, n)
    tn = 1024
    grid = (n // tm, ncols // tn)
    h_spec = pl.BlockSpec((tm, d), lambda i, j: (i, 0))
    m_spec = pl.BlockSpec(m.shape, lambda i, j: (0, 0))
    w_spec = pl.BlockSpec((d, tn), lambda i, j: (0, j))
    o_spec = pl.BlockSpec((tm, tn), lambda i, j: (i, j))
    scratch = [pltpu.VMEM((tm, d), BF16)]
    if wdt is None:
        return pl.pallas_call(
            functools.partial(_modmm_kernel, row=row, sh_off=sh_off),
            grid=grid, in_specs=[h_spec, m_spec, w_spec], out_specs=o_spec,
            out_shape=jax.ShapeDtypeStruct((n, ncols), BF16),
            scratch_shapes=scratch, compiler_params=_cparams("parallel", "arbitrary"),
            name="modmm",
        )(h, m, w)
    return pl.pallas_call(
        functools.partial(_modmm_dt_kernel, row=row, sh_off=sh_off),
        grid=grid,
        in_specs=[h_spec, m_spec, w_spec, pl.BlockSpec((d, DT_PAD), lambda i, j: (0, 0))],
        out_specs=[o_spec, pl.BlockSpec((tm, DT_PAD), lambda i, j: (i, 0))],
        out_shape=[jax.ShapeDtypeStruct((n, ncols), BF16), jax.ShapeDtypeStruct((n, DT_PAD), F32)],
        scratch_shapes=scratch, compiler_params=_cparams("parallel", "arbitrary"),
        name="modmm_dt",
    )(h, m, w, wdt)


def _s5_kernel(*refs, ls):
    x_refs = refs[:S5_CHUNK]
    (wv_ref, rdt_ref, tab_ref, q_ref, s0_ref, y_ref, fin_ref,
     x16_ref, v_ref, sall_ref, wexp_ref, rdexp_ref) = refs[S5_CHUNK:]
    gb = S5_LB
    c_rows = ls * 8
    lane = lax.broadcasted_iota(jnp.int32, (8, 128), 1)
    fwd = lane < 64
    row = lax.broadcasted_iota(jnp.int32, (8, 128), 0)
    zero_tile = jnp.zeros((S5_GROUP, S5_ROW), BF16)

    for t in range(S5_CHUNK):
        x16_ref[:, t * 128:(t + 1) * 128] = x_refs[t][...]

    wexp_ref[...] = jnp.zeros(wexp_ref.shape, BF16)
    for g in range(gb):
        for t in range(S5_CHUNK):
            r0 = t * 128 + g * S5_GROUP
            wexp_ref[r0:r0 + S5_GROUP, :] = wv_ref[g, t * S5_GROUP:(t + 1) * S5_GROUP, :]
        v = jnp.dot(x16_ref[...], wexp_ref[...], preferred_element_type=F32)
        v_ref[g] = v.reshape(ls, 8, S5_ROW)
        for t in range(S5_CHUNK):
            r0 = t * 128 + g * S5_GROUP
            wexp_ref[r0:r0 + S5_GROUP, :] = zero_tile

    qr = [q_ref[g, 0:1, :] for g in range(gb)]
    qi = [q_ref[g, 1:2, :] for g in range(gb)]

    def scan_body(i, carry):
        out = []
        for g in range(gb):
            cr, ci = carry[2 * g], carry[2 * g + 1]
            a = v_ref[g, i]
            b = v_ref[g, ls - 1 - i]
            vr = jnp.where(fwd, a[:, 0:128], b[:, 0:128])
            vi = jnp.where(fwd, a[:, 128:256], b[:, 128:256])
            v_ref[g, i, :, 0:64] = cr[:, 0:64]
            v_ref[g, i, :, 128:192] = ci[:, 0:64]
            v_ref[g, ls - 1 - i, :, 64:128] = cr[:, 64:128]
            v_ref[g, ls - 1 - i, :, 192:256] = ci[:, 64:128]
            out.append(qr[g] * cr - qi[g] * ci + vr)
            out.append(qr[g] * ci + qi[g] * cr + vi)
        return tuple(out)

    zero = jnp.zeros((8, 128), F32)
    tot = lax.fori_loop(0, ls, scan_body, tuple(zero for _ in range(2 * gb)))

    cins = []
    for g in range(gb):
        tr, ti = tot[2 * g], tot[2 * g + 1]
        lr, li = q_ref[g, 2:3, :], q_ref[g, 3:4, :]
        s0r, s0i = s0_ref[g, 0:1, :], s0_ref[g, 1:2, :]
        cf = [(s0r, s0i)]
        for s in range(1, 8):
            pr, pi = cf[-1]
            cf.append((lr * pr - li * pi + tr[s - 1:s, :], lr * pi + li * pr + ti[s - 1:s, :]))
        cb = [(s0r, s0i)]
        for s in range(6, -1, -1):
            pr, pi = cb[-1]
            cb.append((lr * pr - li * pi + tr[s + 1:s + 2, :], lr * pi + li * pr + ti[s + 1:s + 2, :]))
        cb = cb[::-1]
        cin_r, cin_i = zero, zero
        for s in range(8):
            sel = row == s
            cin_r = jnp.where(sel, jnp.where(fwd, cf[s][0], cb[s][0]), cin_r)
            cin_i = jnp.where(sel, jnp.where(fwd, cf[s][1], cb[s][1]), cin_i)
        cins += [cin_r, cin_i]
        ff_r = lr * cf[7][0] - li * cf[7][1] + tr[7:8, :]
        ff_i = lr * cf[7][1] + li * cf[7][0] + ti[7:8, :]
        fb_r = lr * cb[0][0] - li * cb[0][1] + tr[0:1, :]
        fb_i = lr * cb[0][1] + li * cb[0][0] + ti[0:1, :]
        fin_ref[g, 0:1, :] = jnp.where(fwd[0:1], ff_r, fb_r)
        fin_ref[g, 1:2, :] = jnp.where(fwd[0:1], ff_i, fb_i)
        fin_ref[g, 2:8, :] = jnp.zeros((6, 128), F32)

    def fix_body(i, carry):
        out = []
        for g in range(gb):
            dr, di = carry[2 * g], carry[2 * g + 1]
            add_f = jnp.concatenate([jnp.where(fwd, dr, 0.0), jnp.where(fwd, di, 0.0)], axis=1)
            v_ref[g, i] = v_ref[g, i] + add_f
            add_b = jnp.concatenate([jnp.where(fwd, 0.0, dr), jnp.where(fwd, 0.0, di)], axis=1)
            v_ref[g, ls - 1 - i] = v_ref[g, ls - 1 - i] + add_b
            out.append(qr[g] * dr - qi[g] * di)
            out.append(qr[g] * di + qi[g] * dr)
        return tuple(out)

    lax.fori_loop(0, ls, fix_body, tuple(cins))

    for g in range(gb):
        sall_ref[:, g * S5_ROW:(g + 1) * S5_ROW] = v_ref[g].reshape(c_rows, S5_ROW).astype(BF16)

    rdexp_ref[...] = jnp.zeros(rdexp_ref.shape, BF16)
    for tp in range(S5_CHUNK // 2):
        t0 = 2 * tp
        for k in range(2):
            for g in range(gb):
                r0 = k * 128 + g * S5_GROUP
                rdexp_ref[r0:r0 + S5_GROUP, g * S5_ROW:(g + 1) * S5_ROW] = (
                    rdt_ref[g, (t0 + k) * S5_GROUP:(t0 + k + 1) * S5_GROUP, :])
        r0 = (S5_CHUNK - 1 - t0) * 128
        wt = jnp.concatenate([tab_ref[r0:r0 + S5_XW, :], tab_ref[r0 - 128:r0 - 128 + S5_XW, :]], axis=1)
        y2 = jnp.dot(x16_ref[...], wt, preferred_element_type=F32)
        y2 = y2 + lax.dot_general(sall_ref[...], rdexp_ref[...], (((1,), (1,)), ((), ())),
                                  preferred_element_type=F32)
        y_ref[t0] = y2[:, 0:128].astype(BF16)
        y_ref[t0 + 1] = y2[:, 128:256].astype(BF16)


def _s5_scan(u_p, wv, rdt, tab, q, s0):
    c_rows = u_p.shape[0]
    ls = c_rows // 8
    nb = S5_GROUPS // S5_LB
    x_specs = [pl.BlockSpec((c_rows, 128), lambda b, t=t: (0, t * nb + b)) for t in range(S5_CHUNK)]
    gblk = lambda *shape: pl.BlockSpec((S5_LB,) + shape, lambda b: (b,) + (0,) * len(shape))
    return pl.pallas_call(
        functools.partial(_s5_kernel, ls=ls),
        grid=(nb,),
        in_specs=x_specs + [gblk(S5_ROW, S5_ROW), gblk(S5_ROW, S5_ROW),
                            pl.BlockSpec((None, S5_NLAG * 128, 128), lambda b: (b, 0, 0)),
                            gblk(8, 128), gblk(8, 128)],
        out_specs=[pl.BlockSpec((S5_CHUNK, c_rows, 128), lambda b: (0, 0, b)), gblk(8, 128)],
        out_shape=[jax.ShapeDtypeStruct((S5_CHUNK, c_rows, S5_WIDTH), BF16),
                   jax.ShapeDtypeStruct((S5_GROUPS, 8, 128), F32)],
        scratch_shapes=[pltpu.VMEM((c_rows, S5_XW), BF16),
                        pltpu.VMEM((S5_LB, ls, 8, S5_ROW), F32),
                        pltpu.VMEM((c_rows, S5_LB * S5_ROW), BF16),
                        pltpu.VMEM((S5_XW, S5_ROW), BF16),
                        pltpu.VMEM((S5_ROW, S5_LB * S5_ROW), BF16)],
        compiler_params=_cparams("arbitrary"),
        name="s5_scan",
    )(*([u_p] * S5_CHUNK), wv, rdt, tab, q, s0)


def _s5_lag_kernel(c_ref, w_ref, o_ref):
    for g in range(S5_LB):
        o_ref[g] = jnp.dot(c_ref[g], w_ref[g], preferred_element_type=F32, precision=HIGHEST)


def _s5_lag_kernels(cc, ww):
    b = cc.shape[0]
    blk = lambda *shape: pl.BlockSpec((S5_LB,) + shape, lambda i: (i,) + (0,) * len(shape))
    return pl.pallas_call(
        _s5_lag_kernel,
        grid=(b // S5_LB,),
        in_specs=[blk(S5_GROUP, 2 * S5_STATE), blk(2 * S5_STATE, S5_ROW)],
        out_specs=blk(S5_GROUP, S5_ROW),
        out_shape=jax.ShapeDtypeStruct((b, S5_GROUP, S5_ROW), F32),
        compiler_params=_cparams("parallel"),
        name="s5_lag",
    )(cc, ww)


def _s5_operators(lam_re, lam_im, log_step, b_re, b_im, c_re, c_im, d, seg_lens):
    t = S5_CHUNK
    depth = lam_re.shape[0]
    lre = jnp.minimum(lam_re.astype(F32), -1e-4)
    lim = lam_im.astype(F32)
    step = jnp.exp(log_step.astype(F32))[..., None]

    def apow(k):
        mag = jnp.exp(k * lre * step)
        return mag * jnp.cos(k * lim * step), mag * jnp.sin(k * lim * step)

    ab_re, ab_im = apow(1.0)
    den = jnp.square(lre) + jnp.square(lim)
    q_re = ((ab_re - 1.0) * lre + ab_im * lim) / den
    q_im = (ab_im * lre - (ab_re - 1.0) * lim) / den
    bre = b_re.astype(F32)[:, None]
    bim = b_im.astype(F32)[:, None]
    bb_re = q_re[..., None] * bre - q_im[..., None] * bim
    bb_im = q_re[..., None] * bim + q_im[..., None] * bre

    ks = jnp.arange(t + 1, dtype=F32)[:, None, None, None, None]
    pw_re, pw_im = apow(ks)
    pw_re = jnp.moveaxis(pw_re, 0, 3)
    pw_im = jnp.moveaxis(pw_im, 0, 3)
    w_re = pw_re[..., None] * bb_re[:, :, :, None] - pw_im[..., None] * bb_im[:, :, :, None]
    w_im = pw_re[..., None] * bb_im[:, :, :, None] + pw_im[..., None] * bb_re[:, :, :, None]
    cre = c_re.astype(F32)
    cim = c_im.astype(F32)

    cc = jnp.concatenate([cre, -cim], axis=-1).reshape(-1, S5_GROUP, 2 * S5_STATE)
    ww = jnp.concatenate([w_re[:, :, :, :t], w_im[:, :, :, :t]], axis=4)
    ww = ww.transpose(0, 1, 2, 4, 3, 5).reshape(-1, 2 * S5_STATE, S5_ROW)
    kk = _s5_lag_kernels(cc, ww).reshape(depth, 2, S5_GROUPS, S5_GROUP, t, S5_GROUP)
    kf, kb = kk[:, 0], kk[:, 1]
    dd = d.astype(F32).reshape(depth, S5_GROUPS, S5_GROUP)
    mid = kf[:, :, :, 0] + kb[:, :, :, 0] + dd[:, :, :, None] * jnp.eye(S5_GROUP, dtype=F32)
    seq = jnp.concatenate([kf[:, :, :, t - 1:0:-1], mid[:, :, :, None], kb[:, :, :, 1:t]], axis=3)
    seq = seq.transpose(0, 1, 3, 4, 2).reshape(depth, S5_GROUPS // S5_LB, S5_LB, S5_NLAG, S5_GROUP, S5_GROUP)
    seq = seq.transpose(0, 1, 3, 2, 4, 5)
    same = jnp.eye(S5_LB, dtype=F32)[None, None, None, :, None, :, None]
    tab = (seq[:, :, :, :, :, None, :] * same).reshape(depth, S5_GROUPS // S5_LB, S5_NLAG * 128, 128).astype(BF16)

    def sm(arr, dr, rev):
        a = arr[:, dr, :, :t]
        a = a[:, :, ::-1] if rev else a
        return a.transpose(0, 1, 2, 4, 3).reshape(depth, S5_GROUPS, S5_ROW, S5_STATE)

    wv = jnp.concatenate([sm(w_re, 0, True), sm(w_re, 1, False), sm(w_im, 0, True), sm(w_im, 1, False)],
                         axis=3).astype(BF16)

    g_re = cre[:, :, :, None] * pw_re[:, :, :, :, None] - cim[:, :, :, None] * pw_im[:, :, :, :, None]
    g_im = cre[:, :, :, None] * pw_im[:, :, :, :, None] + cim[:, :, :, None] * pw_re[:, :, :, :, None]

    def rd(arr, dr, rev):
        a = arr[:, dr, :, 1:t + 1]
        a = a[:, :, ::-1] if rev else a
        return a.reshape(depth, S5_GROUPS, S5_ROW, S5_STATE)

    rdt = jnp.concatenate([rd(g_re, 0, False), rd(g_re, 1, True), -rd(g_im, 0, False), -rd(g_im, 1, True)],
                          axis=3).astype(BF16)

    def lanes(re_im):
        return jnp.concatenate([re_im[:, 0], re_im[:, 1]], axis=-1)[:, :, None, :]

    c_r, c_i = apow(float(t))
    qs = []
    for ls in seg_lens:
        s_r, s_i = apow(float(t * ls))
        qs.append(jnp.concatenate([lanes(c_r), lanes(c_i), lanes(s_r), lanes(s_i),
                                   jnp.zeros((depth, S5_GROUPS, 4, 128), F32)], axis=2))
    return wv, rdt, tab, qs


def _s5_rows_in(u):
    n = u.shape[0]
    ls = n // (8 * S5_CHUNK)
    return u.reshape(8, ls, S5_CHUNK * S5_WIDTH).transpose(1, 0, 2).reshape(8 * ls, S5_CHUNK * S5_WIDTH)


def _s5_rows_out(y):
    t, c_rows, w = y.shape
    ls = c_rows // 8
    return y.reshape(t, ls, 8, w).transpose(2, 1, 0, 3).reshape(c_rows * t, w)


def _glu_kernel(y_ref, wa_ref, wb_ref, o_ref):
    gl = _gelu_tanh(y_ref[...].astype(F32)).astype(BF16)
    a = jnp.dot(gl, wa_ref[...], preferred_element_type=F32)
    b = jnp.dot(gl, wb_ref[...], preferred_element_type=F32)
    o_ref[...] = (a * _sigmoid(b)).astype(BF16)


def _s5_glu(y, w_glu):
    n, k = y.shape
    tm = min(1024, n)
    return pl.pallas_call(
        _glu_kernel,
        grid=(n // tm,),
        in_specs=[pl.BlockSpec((tm, k), lambda i: (i, 0)),
                  pl.BlockSpec((k, S5_WIDTH), lambda i: (0, 0)),
                  pl.BlockSpec((k, S5_WIDTH), lambda i: (0, 1))],
        out_specs=pl.BlockSpec((tm, S5_WIDTH), lambda i: (i, 0)),
        out_shape=jax.ShapeDtypeStruct((n, S5_WIDTH), BF16),
        compiler_params=_cparams("parallel"),
        name="s5_glu",
    )(y, w_glu, w_glu)


def _ssdconv_kernel(x_ref, xp_ref, xn_ref, w_ref, b_ref, o_ref, *, nt):
    i = pl.program_id(0)
    x = x_ref[...].astype(F32)
    tr = x.shape[0]
    prev = jnp.where(i > 0, xp_ref[...].astype(F32)[15:16, :], 0.0)
    nxt = jnp.where(i < nt - 1, xn_ref[...].astype(F32)[0:1, :], 0.0)
    ri = lax.broadcasted_iota(jnp.int32, x.shape, 0)
    x_m1 = jnp.where(ri == 0, prev, pltpu.roll(x, 1, 0))
    x_p1 = jnp.where(ri == tr - 1, nxt, pltpu.roll(x, tr - 1, 0))
    y = w_ref[0:1, :] * x_m1 + w_ref[1:2, :] * x + w_ref[2:3, :] * x_p1 + b_ref[...]
    o_ref[...] = _silu(y).astype(BF16)


def _ssd_conv(pm, conv_w, conv_b):
    n = pm.shape[0]
    tr = min(512, n)
    tc = 1024
    nt = n // tr
    hb = tr // 16
    return pl.pallas_call(
        functools.partial(_ssdconv_kernel, nt=nt),
        grid=(nt, SSD_CONV_CH // tc),
        in_specs=[pl.BlockSpec((tr, tc), lambda i, j: (i, j)),
                  pl.BlockSpec((16, tc), lambda i, j: (jnp.maximum(i * hb - 1, 0), j)),
                  pl.BlockSpec((16, tc), lambda i, j: (jnp.minimum((i + 1) * hb, n // 16 - 1), j)),
                  pl.BlockSpec((3, tc), lambda i, j: (0, j)),
                  pl.BlockSpec((1, tc), lambda i, j: (0, j))],
        out_specs=pl.BlockSpec((tr, tc), lambda i, j: (i, j)),
        out_shape=jax.ShapeDtypeStruct((n, SSD_CONV_CH), BF16),
        compiler_params=_cparams("parallel", "parallel"),
        name="ssd_conv",
    )(pm, pm, pm, conv_w, conv_b)


def _head_expand(v, mode):
    nh = 2 * SSD_HEADS
    rows = lax.broadcasted_iota(jnp.int32, (2 * nh, SSD_WIDTH), 0) & (nh - 1)
    head = lax.broadcasted_iota(jnp.int32, (2 * nh, SSD_WIDTH), 1) >> 6
    if mode == 'fwd':
        sel = rows == head
    elif mode == 'bwd':
        sel = rows == head + SSD_HEADS
    else:
        sel = (rows & (SSD_HEADS - 1)) == head
    onehot = jnp.where(sel, 1.0, 0.0).astype(BF16)
    hi = v.astype(BF16)
    lo = (v - hi.astype(F32)).astype(BF16)
    return jnp.dot(jnp.concatenate([hi, lo], axis=1), onehot, preferred_element_type=F32)


def _tri_masks(t):
    r = lax.broadcasted_iota(jnp.int32, (t, t), 0)
    c = lax.broadcasted_iota(jnp.int32, (t, t), 1)
    return r, c


def _ssd_fwd_kernel(xc_ref, dt_ref, dtt_ref, bias_ref, biast_ref, a_ref, at_ref, s0_ref,
                    y_ref, sfin_ref, s_ref, *, nc):
    t = SSD_CHUNK
    c = pl.program_id(0)

    @pl.when(c == 0)
    def _():
        s_ref[...] = s0_ref[...]

    dt = _softplus(dt_ref[:, 0:2 * SSD_HEADS] + bias_ref[...])
    dtt = _softplus(dtt_ref[...] + biast_ref[...])
    la = dt * a_ref[...]
    lat = dtt * at_ref[...]
    r, cc = _tri_masks(t)
    lo_incl = jnp.where(r >= cc, 1.0, 0.0)
    up_incl = jnp.where(r <= cc, 1.0, 0.0)
    cum_lo = jnp.dot(lo_incl, la, preferred_element_type=F32, precision=HIGHEST)
    cum_up = jnp.dot(up_incl, la, preferred_element_type=F32, precision=HIGHEST)
    cumt_f = jnp.dot(lat, up_incl, preferred_element_type=F32, precision=HIGHEST)
    cumt_b = jnp.dot(lat, lo_incl, preferred_element_type=F32, precision=HIGHEST)
    lo = r > cc
    diag = r == cc

    end_f = cum_lo[t - 1:t, :]
    exp_f = _head_expand(jnp.exp(cum_lo), 'fwd')
    w_f = _head_expand(jnp.exp(end_f - cum_lo) * dt, 'fwd')
    lane = lax.broadcasted_iota(jnp.int32, (t, 128), 1)

    for g in range(SSD_GROUPS):
        gs = slice(g * SSD_GW, (g + 1) * SSD_GW)
        bg = xc_ref[:, SSD_WIDTH + g * SSD_STATE:SSD_WIDTH + (g + 1) * SSD_STATE]
        cg = xc_ref[:, SSD_WIDTH + (SSD_GROUPS + g) * SSD_STATE:SSD_WIDTH + (SSD_GROUPS + g + 1) * SSD_STATE]
        cb = lax.dot_general(cg, bg, (((1,), (1,)), ((), ())), preferred_element_type=F32)
        cb = jnp.where(diag, 0.0, cb)
        bt = bg.astype(F32).T.astype(BF16)
        xg = xc_ref[:, gs]
        sg = s_ref[g]
        y_int = exp_f[:, gs] * jnp.dot(cg, sg.astype(BF16), preferred_element_type=F32)
        xw = (xg.astype(F32) * w_f[:, gs]).astype(BF16)
        s_ref[g] = sg * exp_f[t - 1:t, gs] + jnp.dot(bt, xw, preferred_element_type=F32)
        for p in range(SSD_GW // 128):
            xp = xg[:, p * 128:(p + 1) * 128]
            acc = y_int[:, p * 128:(p + 1) * 128]
            for q in range(2):
                hh = g * (SSD_HEADS // SSD_GROUPS) + 2 * p + q
                hb = SSD_HEADS + hh
                csel = jnp.where(lo, cum_lo[:, hh:hh + 1], cum_up[:, hb:hb + 1])
                rsel = jnp.where(lo, cumt_f[hh:hh + 1, :], cumt_b[hb:hb + 1, :])
                dsel = jnp.where(lo, dtt[hh:hh + 1, :], dtt[hb:hb + 1, :])
                sc = (cb * jnp.exp(csel - rsel) * dsel).astype(BF16)
                xh = jnp.where((lane < 64) if q == 0 else (lane >= 64), xp, jnp.zeros_like(xp))
                acc = acc + jnp.dot(sc, xh, preferred_element_type=F32)
            y_ref[:, g * SSD_GW + p * 128:g * SSD_GW + (p + 1) * 128] = acc

    @pl.when(c == nc - 1)
    def _():
        sfin_ref[...] = s_ref[...]


def _ssd_bwd_kernel(xc_ref, dt_ref, z_ref, yp_ref, bias_ref, a_ref, dvec_ref, nw_ref, s0_ref,
                    o_ref, sfin_ref, s_ref, *, nc):
    t = SSD_CHUNK
    c = pl.program_id(0)

    @pl.when(c == 0)
    def _():
        s_ref[...] = s0_ref[...]

    dt = _softplus(dt_ref[:, 0:2 * SSD_HEADS] + bias_ref[...])
    la = dt * a_ref[...]
    r, cc = _tri_masks(t)
    up_incl = jnp.where(r <= cc, 1.0, 0.0)
    cum_up = jnp.dot(up_incl, la, preferred_element_type=F32, precision=HIGHEST)
    exp_b = _head_expand(jnp.exp(cum_up), 'bwd')
    w_b = _head_expand(jnp.exp(cum_up[0:1, :] - cum_up) * dt, 'bwd')

    bgs = [xc_ref[:, SSD_WIDTH + g * SSD_STATE:SSD_WIDTH + (g + 1) * SSD_STATE] for g in range(SSD_GROUPS)]
    cgs = [xc_ref[:, SSD_WIDTH + (SSD_GROUPS + g) * SSD_STATE:SSD_WIDTH + (SSD_GROUPS + g + 1) * SSD_STATE]
           for g in range(SSD_GROUPS)]
    grp = (lax.broadcasted_iota(jnp.int32, (t, 2 * SSD_HEADS), 1) & (SSD_HEADS - 1)) >> 3
    cbd = jnp.zeros((t, 2 * SSD_HEADS), F32)
    for g in range(SSD_GROUPS):
        dg = jnp.sum(cgs[g].astype(F32) * bgs[g].astype(F32), axis=-1, keepdims=True)
        cbd = jnp.where(grp == g, dg, cbd)
    coef = _head_expand(dt * cbd, 'both') + dvec_ref[...]

    for g in range(SSD_GROUPS):
        gs = slice(g * SSD_GW, (g + 1) * SSD_GW)
        bt = bgs[g].astype(F32).T.astype(BF16)
        xg = xc_ref[:, gs].astype(F32)
        sg = s_ref[g]
        y_int = exp_b[:, gs] * jnp.dot(cgs[g], sg.astype(BF16), preferred_element_type=F32)
        xw = (xg * w_b[:, gs]).astype(BF16)
        s_ref[g] = sg * exp_b[0:1, gs] + jnp.dot(bt, xw, preferred_element_type=F32)
        y = yp_ref[:, gs] + y_int + xg * coef[:, gs]
        hcur = y * _silu(z_ref[:, gs].astype(F32))
        ms = jnp.mean(hcur * hcur, axis=-1, keepdims=True)
        o_ref[:, gs] = (hcur * lax.rsqrt(ms + RMS_EPS) * nw_ref[:, gs]).astype(BF16)

    @pl.when(c == nc - 1)
    def _():
        sfin_ref[...] = s_ref[...]


def _ssd_scan(xc, dt_raw, dt_t, pm, prm, s0_f, s0_b):
    n = xc.shape[0]
    t = SSD_CHUNK
    nc = n // t
    full = lambda a: pl.BlockSpec(a.shape, lambda c: (0,) * a.ndim)
    s_shape = (SSD_GROUPS, SSD_STATE, SSD_GW)
    s_spec = pl.BlockSpec(s_shape, lambda c: (0, 0, 0))
    y_part, sfin_f = pl.pallas_call(
        functools.partial(_ssd_fwd_kernel, nc=nc),
        grid=(nc,),
        in_specs=[pl.BlockSpec((t, SSD_CONV_CH), lambda c: (c, 0)),
                  pl.BlockSpec((t, DT_PAD), lambda c: (c, 0)),
                  pl.BlockSpec((2 * SSD_HEADS, t), lambda c: (0, c)),
                  full(prm['bias']), full(prm['bias_t']), full(prm['a']), full(prm['a_t']), s_spec],
        out_specs=[pl.BlockSpec((t, SSD_WIDTH), lambda c: (c, 0)), s_spec],
        out_shape=[jax.ShapeDtypeStruct((n, SSD_WIDTH), F32), jax.ShapeDtypeStruct(s_shape, F32)],
        scratch_shapes=[pltpu.VMEM(s_shape, F32)],
        compiler_params=_cparams("arbitrary"),
        name="ssd_fwd",
    )(xc, dt_raw, dt_t, prm['bias'], prm['bias_t'], prm['a'], prm['a_t'], s0_f)
    rev = lambda c: (nc - 1 - c, 0)
    z_blk = PM_Z // SSD_WIDTH
    yd, sfin_b = pl.pallas_call(
        functools.partial(_ssd_bwd_kernel, nc=nc),
        grid=(nc,),
        in_specs=[pl.BlockSpec((t, SSD_CONV_CH), rev),
                  pl.BlockSpec((t, DT_PAD), rev),
                  pl.BlockSpec((t, SSD_WIDTH), lambda c: (nc - 1 - c, z_blk)),
                  pl.BlockSpec((t, SSD_WIDTH), rev),
                  full(prm['bias']), full(prm['a']), full(prm['dvec']), full(prm['norm_w']), s_spec],
        out_specs=[pl.BlockSpec((t, SSD_WIDTH), rev), s_spec],
        out_shape=[jax.ShapeDtypeStruct((n, SSD_WIDTH), BF16), jax.ShapeDtypeStruct(s_shape, F32)],
        scratch_shapes=[pltpu.VMEM(s_shape, F32)],
        compiler_params=_cparams("arbitrary"),
        name="ssd_bwd",
    )(xc, dt_raw, pm, y_part, prm['bias'], prm['a'], prm['dvec'], prm['norm_w'], s0_b)
    return yd, sfin_f, sfin_b


def _merge_kernel(y5_ref, yd_ref, w5_ref, wd_ref, g5_ref, gd_ref, o_ref):
    a = jnp.dot(y5_ref[...], w5_ref[...], preferred_element_type=F32)
    b = jnp.dot(yd_ref[...], wd_ref[...], preferred_element_type=F32)
    o = _sigmoid(g5_ref[...].astype(F32)) * a + _sigmoid(gd_ref[...].astype(F32)) * b
    o_ref[...] = o.astype(BF16)


def _merge(y5, yd, w5, wd, pm):
    n = y5.shape[0]
    tm = min(1024, n)
    tn = 1024
    g5_blk, gd_blk = PM_G5 // tn, PM_GD // tn
    return pl.pallas_call(
        _merge_kernel,
        grid=(n // tm, D_MODEL // tn),
        in_specs=[pl.BlockSpec((tm, S5_WIDTH), lambda i, j: (i, 0)),
                  pl.BlockSpec((tm, SSD_WIDTH), lambda i, j: (i, 0)),
                  pl.BlockSpec((S5_WIDTH, tn), lambda i, j: (0, j)),
                  pl.BlockSpec((SSD_WIDTH, tn), lambda i, j: (0, j)),
                  pl.BlockSpec((tm, tn), lambda i, j: (i, g5_blk + j)),
                  pl.BlockSpec((tm, tn), lambda i, j: (i, gd_blk + j))],
        out_specs=pl.BlockSpec((tm, tn), lambda i, j: (i, j)),
        out_shape=jax.ShapeDtypeStruct((n, D_MODEL), BF16),
        compiler_params=_cparams("parallel", "parallel"),
        name="merge",
    )(y5, yd, w5, wd, pm, pm)


def _proj_ln_kernel(x_ref, w_ref, h_ref, m_ref, lg_ref, lb_ref, o_ref, acc_ref, *, row, g_off, nk, alpha):
    k = pl.program_id(1)
    part = jnp.dot(x_ref[...], w_ref[...], preferred_element_type=F32)

    @pl.when(k == 0)
    def _():
        acc_ref[...] = part

    @pl.when(k > 0)
    def _():
        acc_ref[...] += part

    @pl.when(k == nk - 1)
    def _():
        d = h_ref.shape[1]
        gate = m_ref[row:row + 1, g_off:g_off + d]
        o_ref[...] = _layer_norm(alpha * h_ref[...] + gate * acc_ref[...], lg_ref[...], lb_ref[...])


def _proj_ln(x, w, h, m, ln_g, ln_b, *, row, g_off, tk, alpha):
    n, kdim = x.shape
    d = h.shape[1]
    tm = min(512, n)
    nk = kdim // tk
    return pl.pallas_call(
        functools.partial(_proj_ln_kernel, row=row, g_off=g_off, nk=nk, alpha=alpha),
        grid=(n // tm, nk),
        in_specs=[pl.BlockSpec((tm, tk), lambda i, k: (i, k)),
                  pl.BlockSpec((tk, d), lambda i, k: (k, 0)),
                  pl.BlockSpec((tm, d), lambda i, k: (i, 0)),
                  pl.BlockSpec(m.shape, lambda i, k: (0, 0)),
                  pl.BlockSpec((1, d), lambda i, k: (0, 0)),
                  pl.BlockSpec((1, d), lambda i, k: (0, 0))],
        out_specs=pl.BlockSpec((tm, d), lambda i, k: (i, 0)),
        out_shape=jax.ShapeDtypeStruct((n, d), F32),
        scratch_shapes=[pltpu.VMEM((tm, d), F32)],
        compiler_params=_cparams("parallel", "arbitrary"),
        name="proj_ln",
    )(x, w, h, m, ln_g, ln_b)


FFN_HALO = 128


def _ffnconv_kernel(g_ref, gp_ref, gn_ref, v_ref, w_ref, b_ref, o_ref, ext_ref, *, nt, grid_mode):
    i = pl.program_id(0)
    tr = g_ref.shape[0]
    hl = FFN_HALO
    ext_ref[0:hl, :] = jnp.where(i > 0, gp_ref[...].astype(F32), 0.0)
    ext_ref[hl:hl + tr, :] = g_ref[...].astype(F32)
    ext_ref[hl + tr:hl + tr + hl, :] = jnp.where(i < nt - 1, gn_ref[...].astype(F32), 0.0)

    def tap(dy, dx):
        wrow = (dy + 1) * 3 + (dx + 1)
        return w_ref[wrow:wrow + 1, :] * ext_ref[pl.ds(hl + GRID_W * dy + dx, tr), :]

    if grid_mode:
        col = lax.broadcasted_iota(jnp.int32, (tr, g_ref.shape[1]), 0) & (GRID_W - 1)
        acc_c = tap(-1, 0) + tap(0, 0) + tap(1, 0)
        acc_l = tap(-1, -1) + tap(0, -1) + tap(1, -1)
        acc_r = tap(-1, 1) + tap(0, 1) + tap(1, 1)
        y = acc_c + jnp.where(col != 0, acc_l, 0.0) + jnp.where(col != GRID_W - 1, acc_r, 0.0)
    else:
        y = tap(0, -1) + tap(0, 0) + tap(0, 1)
    y = y + b_ref[...]
    o_ref[...] = (_silu(y) * v_ref[...].astype(F32)).astype(BF16)


def _ffn_conv(up, conv_w9, conv_b, *, grid_mode):
    n = up.shape[0]
    tr = min(1024, n)
    tc = 512
    nt = n // tr
    hb = tr // FFN_HALO
    ncb = D_FF // tc
    return pl.pallas_call(
        functools.partial(_ffnconv_kernel, nt=nt, grid_mode=grid_mode),
        grid=(nt, ncb),
        in_specs=[pl.BlockSpec((tr, tc), lambda i, j: (i, j)),
                  pl.BlockSpec((FFN_HALO, tc), lambda i, j: (jnp.maximum(i * hb - 1, 0), j)),
                  pl.BlockSpec((FFN_HALO, tc), lambda i, j: (jnp.minimum((i + 1) * hb, n // FFN_HALO - 1), j)),
                  pl.BlockSpec((tr, tc), lambda i, j: (i, ncb + j)),
                  pl.BlockSpec((9, tc), lambda i, j: (0, j)),
                  pl.BlockSpec((1, tc), lambda i, j: (0, j))],
        out_specs=pl.BlockSpec((tr, tc), lambda i, j: (i, j)),
        out_shape=jax.ShapeDtypeStruct((n, D_FF), BF16),
        scratch_shapes=[pltpu.VMEM((tr + 2 * FFN_HALO, tc), F32)],
        compiler_params=_cparams("parallel", "parallel"),
        name="ffn_conv",
    )(up, up, up, up, conv_w9, conv_b)


def _mixer(h, m, row, lw, q, s5_init, ssd_init):
    pm, dt_raw = _modmm(h, m, lw['w_main'], row=row, sh_off=0, wdt=lw['w_dt'])
    u_p = _s5_rows_in(pm[:, PM_U:PM_U + S5_WIDTH])
    y5t, s5_fin = _s5_scan(u_p, lw['s5_wv'], lw['s5_rdt'], lw['s5_tab'], q, s5_init)
    y5 = _s5_glu(_s5_rows_out(y5t), lw['s5_w_glu'])
    xc = _ssd_conv(pm, lw['ssd_conv_w'], lw['ssd_conv_b'])
    dt_t = dt_raw[:, 0:2 * SSD_HEADS].T
    yd, sf, sb = _ssd_scan(xc, dt_raw, dt_t, pm, lw['ssd'], ssd_init[0], ssd_init[1])
    return pm, y5, yd, s5_fin, (sf, sb)


def _rest_of_layer(h, m, row, lw, pm, y5, yd, alpha, *, grid_mode):
    d = D_MODEL
    merged = _merge(y5, yd, lw['s5_w_proj'], lw['ssd_w_proj'], pm)
    h1 = _proj_ln(merged, lw['w_out'], h, m, lw['ln1_g'], lw['ln1_b'], row=row, g_off=2 * d, tk=d, alpha=alpha)
    up = _modmm(h1, m, lw['w_up'], row=row, sh_off=3 * d)
    act = _ffn_conv(up, lw['ffn_conv_w'], lw['ffn_conv_b'], grid_mode=grid_mode)
    return _proj_ln(act, lw['w_down'], h1, m, lw['ln2_g'], lw['ln2_b'], row=row, g_off=5 * d, tk=D_FF // 4, alpha=alpha)


def kernel(x, c, ctx, c_ctx, w_ada, b_ada, w_in, s5_lam_re, s5_lam_im, s5_log_step, s5_b_re, s5_b_im, s5_c_re, s5_c_im, s5_d, s5_w_glu, s5_w_proj, ssd_conv_w, ssd_conv_b, ssd_dt_bias, ssd_a_log, ssd_d, ssd_norm_w, ssd_w_proj, w_out, ln1_g, ln1_b, w_up, ffn_conv_w, ffn_conv_b, w_down, ln2_g, ln2_b):
    depth = w_ada.shape[0]
    alpha = float((2 * depth) ** 0.25)
    n_lat, n_ctx = x.shape[1], ctx.shape[1]
    h_lat, h_ctx = x[0], ctx[0]

    cc = jnp.zeros((8, D_MODEL), F32).at[0].set(c[0]).at[1].set(c_ctx)
    mods = _adaln(cc, w_ada, b_ada)

    seg = (n_lat // (8 * S5_CHUNK), n_ctx // (8 * S5_CHUNK))
    s5_wv, s5_rdt, s5_tab, (q_lat, q_ctx) = _s5_operators(s5_lam_re, s5_lam_im, s5_log_step, s5_b_re, s5_b_im,
                                                          s5_c_re, s5_c_im, s5_d, seg)

    w_main = jnp.concatenate([w_in[:, :, COL_XBC:COL_DT], w_in[:, :, :COL_XBC], w_in[:, :, COL_Z:]], axis=2).astype(BF16)
    w_dt = jnp.pad(w_in[:, :, COL_DT:COL_Z], ((0, 0), (0, 0), (0, DT_PAD - 2 * SSD_HEADS))).astype(BF16)
    a64 = (-jnp.exp(ssd_a_log.astype(F32))).reshape(depth, 1, 2 * SSD_HEADS)
    bias64 = ssd_dt_bias.astype(F32).reshape(depth, 1, 2 * SSD_HEADS)
    dvec = jnp.repeat(ssd_d.astype(F32), SSD_HEAD_DIM, axis=1).reshape(depth, 1, SSD_WIDTH)

    s5_zero = jnp.zeros((S5_GROUPS, 8, 128), F32)
    ssd_zero = jnp.zeros((SSD_GROUPS, SSD_STATE, SSD_GW), F32)

    for i in range(depth):
        last = i == depth - 1
        lw = {
            'w_main': w_main[i], 'w_dt': w_dt[i], 's5_wv': s5_wv[i], 's5_rdt': s5_rdt[i], 's5_tab': s5_tab[i],
            's5_w_glu': s5_w_glu[i].astype(BF16), 's5_w_proj': s5_w_proj[i].astype(BF16),
            'ssd_conv_w': ssd_conv_w[i], 'ssd_conv_b': ssd_conv_b[i].reshape(1, SSD_CONV_CH),
            'ssd': {'bias': bias64[i], 'bias_t': bias64[i].reshape(2 * SSD_HEADS, 1),
                    'a': a64[i], 'a_t': a64[i].reshape(2 * SSD_HEADS, 1),
                    'dvec': dvec[i], 'norm_w': ssd_norm_w[i].astype(F32).reshape(1, SSD_WIDTH)},
            'ssd_w_proj': ssd_w_proj[i].astype(BF16), 'w_out': w_out[i].astype(BF16),
            'ln1_g': ln1_g[i].reshape(1, D_MODEL), 'ln1_b': ln1_b[i].reshape(1, D_MODEL),
            'w_up': w_up[i].astype(BF16), 'ffn_conv_w': ffn_conv_w[i].reshape(9, D_FF),
            'ffn_conv_b': ffn_conv_b[i].reshape(1, D_FF), 'w_down': w_down[i].astype(BF16),
            'ln2_g': ln2_g[i].reshape(1, D_MODEL), 'ln2_b': ln2_b[i].reshape(1, D_MODEL),
        }
        m = mods[i]
        pm_c, y5_c, yd_c, s5_fin, ssd_fin = _mixer(h_ctx, m, 1, lw, q_ctx[i], s5_zero, (ssd_zero, ssd_zero))
        pm_l, y5_l, yd_l, _, _ = _mixer(h_lat, m, 0, lw, q_lat[i], s5_fin, ssd_fin)
        h_lat = _rest_of_layer(h_lat, m, 0, lw, pm_l, y5_l, yd_l, alpha, grid_mode=True)
        if not last:
            h_ctx = _rest_of_layer(h_ctx, m, 1, lw, pm_c, y5_c, yd_c, alpha, grid_mode=False)
    return h_lat[None]
```

```python
import functools

import jax
import jax.numpy as jnp
from jax import lax
from jax.experimental import pallas as pl
from jax.experimental.pallas import tpu as pltpu

F32 = jnp.float32
BF16 = jnp.bfloat16
HIGHEST = lax.Precision.HIGHEST

D_MODEL = 2048
GRID_W = 64
S5_WIDTH = 1024
S5_GROUP = 16
S5_GROUPS = S5_WIDTH // S5_GROUP
S5_STATE = 64
S5_CHUNK = 16
S5_ROW = S5_CHUNK * S5_GROUP
S5_LB = 128 // S5_GROUP
S5_XW = S5_CHUNK * 128
S5_NLAG = 2 * S5_CHUNK - 1
SSD_WIDTH = 2048
SSD_HEAD_DIM = 64
SSD_HEADS = SSD_WIDTH // SSD_HEAD_DIM
SSD_GROUPS = 4
SSD_STATE = 128
SSD_CHUNK = 128
SSD_GW = SSD_WIDTH // SSD_GROUPS
SSD_CONV_CH = SSD_WIDTH + 2 * SSD_GROUPS * SSD_STATE
D_FF = 5632
COL_XBC = S5_WIDTH
COL_DT = COL_XBC + SSD_CONV_CH
COL_Z = COL_DT + 2 * SSD_HEADS
COL_GATES = COL_Z + SSD_WIDTH
LN_EPS = 1e-5
RMS_EPS = 1e-5

PM_XBC = 0
PM_U = SSD_CONV_CH
PM_Z = PM_U + S5_WIDTH
PM_G5 = PM_Z + SSD_WIDTH
PM_GD = PM_G5 + D_MODEL
PM_COLS = PM_GD + D_MODEL
DT_PAD = 128

VMEM_LIMIT = 56 * 1024 * 1024


def _cparams(*sem):
    return pltpu.CompilerParams(dimension_semantics=sem, vmem_limit_bytes=VMEM_LIMIT)


def _sigmoid(x):
    return 1.0 / (1.0 + jnp.exp(-x))


def _silu(x):
    return x * _sigmoid(x)


def _gelu_tanh(x):
    return 0.5 * x * (1.0 + jnp.tanh(0.7978845608028654 * (x + 0.044715 * x * x * x)))


def _softplus(x):
    return jnp.maximum(x, 0.0) + jnp.log1p(jnp.exp(-jnp.abs(x)))


def _layer_norm(t, gam, bet):
    mu = jnp.mean(t, axis=-1, keepdims=True)
    d = t - mu
    var = jnp.mean(d * d, axis=-1, keepdims=True)
    return d * lax.rsqrt(var + LN_EPS) * gam + bet


def _adaln_kernel(c_ref, w_ref, b_ref, o_ref):
    s = _silu(c_ref[...]).astype(BF16)
    o_ref[0] = jnp.dot(s, w_ref[0].astype(BF16), preferred_element_type=F32) + b_ref[0]


def _adaln(cc, w_ada, b_ada):
    depth, d, n6 = w_ada.shape
    tn = 1024
    return pl.pallas_call(
        _adaln_kernel,
        grid=(depth, n6 // tn),
        in_specs=[pl.BlockSpec((8, d), lambda l, j: (0, 0)),
                  pl.BlockSpec((1, d, tn), lambda l, j: (l, 0, j)),
                  pl.BlockSpec((1, 1, tn), lambda l, j: (l, 0, j))],
        out_specs=pl.BlockSpec((1, 8, tn), lambda l, j: (l, 0, j)),
        out_shape=jax.ShapeDtypeStruct((depth, 8, n6), F32),
        compiler_params=_cparams("parallel", "parallel"),
        name="adaln",
    )(cc, w_ada, b_ada.reshape(depth, 1, n6))


REPACK_TN = 1024


def _repack_kernel(a_ref, b_ref, o_ref, *, first_shifted):
    j = pl.program_id(1)
    a = a_ref[...]
    sh = 2 * SSD_HEADS
    shifted = jnp.concatenate([a[:, sh:], b_ref[:, 0:sh]], axis=1)
    o_ref[...] = jnp.where(j >= first_shifted, shifted, a).astype(BF16)


def _repack_w_in(w_in):
    depth, d, _ = w_in.shape
    tn = REPACK_TN
    n_xbc, n_u = SSD_CONV_CH // tn, S5_WIDTH // tn
    first_shifted = n_xbc + n_u

    def a_idx(l, j):
        src = jnp.where(j < n_xbc, j + COL_XBC // tn, jnp.where(j < first_shifted, j - n_xbc, j))
        return (l, 0, src)

    def b_idx(l, j):
        return (l, 0, jnp.where(j >= first_shifted, (j + 1) * (tn // 128), 0))

    return pl.pallas_call(
        functools.partial(_repack_kernel, first_shifted=first_shifted),
        grid=(depth, PM_COLS // tn),
        in_specs=[pl.BlockSpec((None, d, tn), a_idx), pl.BlockSpec((None, d, 128), b_idx)],
        out_specs=pl.BlockSpec((None, d, tn), lambda l, j: (l, 0, j)),
        out_shape=jax.ShapeDtypeStruct((depth, d, PM_COLS), BF16),
        compiler_params=_cparams("parallel", "parallel"),
        name="repack_w_in",
    )(w_in, w_in)


def _modulated(h_ref, m_ref, row, sh_off):
    d = h_ref.shape[1]
    sh = m_ref[row:row + 1, sh_off:sh_off + d]
    sc = m_ref[row:row + 1, sh_off + d:sh_off + 2 * d]
    return (h_ref[...] * (1.0 + sc) + sh).astype(BF16)


def _modmm_kernel(h_ref, m_ref, w_ref, o_ref, xs_ref, *, row, sh_off):
    @pl.when(pl.program_id(1) == 0)
    def _():
        xs_ref[...] = _modulated(h_ref, m_ref, row, sh_off)

    o_ref[...] = jnp.dot(xs_ref[...], w_ref[...], preferred_element_type=F32).astype(o_ref.dtype)


def _modmm_dt_kernel(h_ref, m_ref, w_ref, wdt_ref, o_ref, dt_ref, ut_ref, xs_ref, *, row, sh_off, u_tile):
    j = pl.program_id(1)

    @pl.when(j == 0)
    def _():
        xs = _modulated(h_ref, m_ref, row, sh_off)
        xs_ref[...] = xs
        dt_ref[...] = jnp.dot(xs, wdt_ref[...], preferred_element_type=F32)

    res = jnp.dot(xs_ref[...], w_ref[...], preferred_element_type=F32).astype(BF16)
    o_ref[...] = res

    @pl.when(j == u_tile)
    def _():
        r3 = res.reshape(res.shape[0] // S5_CHUNK, S5_CHUNK, res.shape[1])
        for t in range(S5_CHUNK):
            ut_ref[t] = r3[:, t, :]


def _modmm(h, m, w, layer, *, row, sh_off, wdt=None):
    n, d = h.shape
    ncols = w.shape[2]
    tm = min(1024, n)
    tn = 1024
    grid = (n // tm, ncols // tn)
    h_spec = pl.BlockSpec((tm, d), lambda i, j: (i, 0))
    m_spec = pl.BlockSpec(m.shape, lambda i, j: (0, 0))
    w_spec = pl.BlockSpec((None, d, tn), lambda i, j: (layer, 0, j))
    o_spec = pl.BlockSpec((tm, tn), lambda i, j: (i, j))
    scratch = [pltpu.VMEM((tm, d), BF16)]
    if wdt is None:
        return pl.pallas_call(
            functools.partial(_modmm_kernel, row=row, sh_off=sh_off),
            grid=grid, in_specs=[h_spec, m_spec, w_spec], out_specs=o_spec,
            out_shape=jax.ShapeDtypeStruct((n, ncols), BF16),
            scratch_shapes=scratch, compiler_params=_cparams("parallel", "arbitrary"),
            name="modmm",
        )(h, m, w)
    return pl.pallas_call(
        functools.partial(_modmm_dt_kernel, row=row, sh_off=sh_off, u_tile=PM_U // tn),
        grid=grid,
        in_specs=[h_spec, m_spec, w_spec, pl.BlockSpec((None, d, DT_PAD), lambda i, j: (layer, 0, 0))],
        out_specs=[o_spec, pl.BlockSpec((tm, DT_PAD), lambda i, j: (i, 0)),
                   pl.BlockSpec((S5_CHUNK, tm // S5_CHUNK, S5_WIDTH), lambda i, j: (0, i, 0))],
        out_shape=[jax.ShapeDtypeStruct((n, ncols), BF16), jax.ShapeDtypeStruct((n, DT_PAD), F32),
                   jax.ShapeDtypeStruct((S5_CHUNK, n // S5_CHUNK, S5_WIDTH), BF16)],
        scratch_shapes=scratch, compiler_params=_cparams("parallel", "arbitrary"),
        name="modmm_dt",
    )(h, m, w, wdt)


def _s5_kernel(*refs, ls, interleave):
    x_refs = refs[:S5_CHUNK]
    (wv_ref, rdt_ref, kc_ref, q_ref, s0_ref, y_ref, fin_ref,
     x16_ref, v_ref, sall_ref, wexp_ref, rdexp_ref, tab_ref) = refs[S5_CHUNK:]
    gb = S5_LB
    c_rows = ls * 8
    lane = lax.broadcasted_iota(jnp.int32, (8, 128), 1)
    fwd = lane < 64
    row = lax.broadcasted_iota(jnp.int32, (8, 128), 0)
    zero_tile = jnp.zeros((S5_GROUP, S5_ROW), BF16)

    for t in range(S5_CHUNK):
        x16_ref[:, t * 128:(t + 1) * 128] = x_refs[t][...]

    rep = jnp.where((lax.broadcasted_iota(jnp.int32, (S5_GROUP, 128), 1) & (S5_GROUP - 1))
                    == lax.broadcasted_iota(jnp.int32, (S5_GROUP, 128), 0), 1.0, 0.0).astype(BF16)
    spread = jnp.dot(kc_ref[...], rep, preferred_element_type=F32)
    own = (((lax.broadcasted_iota(jnp.int32, spread.shape, 0) >> 4) & (S5_LB - 1))
           == (lax.broadcasted_iota(jnp.int32, spread.shape, 1) >> 4))
    tab_ref[...] = jnp.where(own, spread, 0.0).astype(BF16)

    wexp_ref[...] = jnp.zeros(wexp_ref.shape, BF16)
    for g in range(gb):
        for t in range(S5_CHUNK):
            r0 = t * 128 + g * S5_GROUP
            wexp_ref[r0:r0 + S5_GROUP, :] = wv_ref[g, t * S5_GROUP:(t + 1) * S5_GROUP, :]
        v = jnp.dot(x16_ref[...], wexp_ref[...], preferred_element_type=F32)
        if interleave:
            v3 = v.reshape(8, ls, S5_ROW)
            v_ref[g] = jnp.stack([v3[s] for s in range(8)], axis=1)
        else:
            v_ref[g] = v.reshape(ls, 8, S5_ROW)
        for t in range(S5_CHUNK):
            r0 = t * 128 + g * S5_GROUP
            wexp_ref[r0:r0 + S5_GROUP, :] = zero_tile

    qr = [q_ref[g, 0:1, :] for g in range(gb)]
    qi = [q_ref[g, 1:2, :] for g in range(gb)]

    def scan_body(i, carry):
        out = []
        for g in range(gb):
            cr, ci = carry[2 * g], carry[2 * g + 1]
            a = v_ref[g, i]
            b = v_ref[g, ls - 1 - i]
            vr = jnp.where(fwd, a[:, 0:128], b[:, 0:128])
            vi = jnp.where(fwd, a[:, 128:256], b[:, 128:256])
            v_ref[g, i, :, 0:64] = cr[:, 0:64]
            v_ref[g, i, :, 128:192] = ci[:, 0:64]
            v_ref[g, ls - 1 - i, :, 64:128] = cr[:, 64:128]
            v_ref[g, ls - 1 - i, :, 192:256] = ci[:, 64:128]
            out.append(qr[g] * cr - qi[g] * ci + vr)
            out.append(qr[g] * ci + qi[g] * cr + vi)
        return tuple(out)

    zero = jnp.zeros((8, 128), F32)
    tot = lax.fori_loop(0, ls, scan_body, tuple(zero for _ in range(2 * gb)))

    cins = []
    for g in range(gb):
        tr, ti = tot[2 * g], tot[2 * g + 1]
        lr, li = q_ref[g, 2:3, :], q_ref[g, 3:4, :]
        s0r, s0i = s0_ref[g, 0:1, :], s0_ref[g, 1:2, :]
        cf = [(s0r, s0i)]
        for s in range(1, 8):
            pr, pi = cf[-1]
            cf.append((lr * pr - li * pi + tr[s - 1:s, :], lr * pi + li * pr + ti[s - 1:s, :]))
        cb = [(s0r, s0i)]
        for s in range(6, -1, -1):
            pr, pi = cb[-1]
            cb.append((lr * pr - li * pi + tr[s + 1:s + 2, :], lr * pi + li * pr + ti[s + 1:s + 2, :]))
        cb = cb[::-1]
        cin_r, cin_i = zero, zero
        for s in range(8):
            sel = row == s
            cin_r = jnp.where(sel, jnp.where(fwd, cf[s][0], cb[s][0]), cin_r)
            cin_i = jnp.where(sel, jnp.where(fwd, cf[s][1], cb[s][1]), cin_i)
        cins += [cin_r, cin_i]
        ff_r = lr * cf[7][0] - li * cf[7][1] + tr[7:8, :]
        ff_i = lr * cf[7][1] + li * cf[7][0] + ti[7:8, :]
        fb_r = lr * cb[0][0] - li * cb[0][1] + tr[0:1, :]
        fb_i = lr * cb[0][1] + li * cb[0][0] + ti[0:1, :]
        fin_ref[g, 0:1, :] = jnp.where(fwd[0:1], ff_r, fb_r)
        fin_ref[g, 1:2, :] = jnp.where(fwd[0:1], ff_i, fb_i)
        fin_ref[g, 2:8, :] = jnp.zeros((6, 128), F32)

    def fix_body(i, carry):
        out = []
        for g in range(gb):
            dr, di = carry[2 * g], carry[2 * g + 1]
            add_f = jnp.concatenate([jnp.where(fwd, dr, 0.0), jnp.where(fwd, di, 0.0)], axis=1)
            v_ref[g, i] = v_ref[g, i] + add_f
            add_b = jnp.concatenate([jnp.where(fwd, 0.0, dr), jnp.where(fwd, 0.0, di)], axis=1)
            v_ref[g, ls - 1 - i] = v_ref[g, ls - 1 - i] + add_b
            out.append(qr[g] * dr - qi[g] * di)
            out.append(qr[g] * di + qi[g] * dr)
        return tuple(out)

    lax.fori_loop(0, ls, fix_body, tuple(cins))

    for g in range(gb):
        sg = v_ref[g]
        if interleave:
            sin = jnp.concatenate([sg[:, s, :] for s in range(8)], axis=0)
        else:
            sin = sg.reshape(c_rows, S5_ROW)
        sall_ref[:, g * S5_ROW:(g + 1) * S5_ROW] = sin.astype(BF16)

    rdexp_ref[...] = jnp.zeros(rdexp_ref.shape, BF16)
    for tp in range(S5_CHUNK // 2):
        t0 = 2 * tp
        for k in range(2):
            for g in range(gb):
                r0 = k * 128 + g * S5_GROUP
                rdexp_ref[r0:r0 + S5_GROUP, g * S5_ROW:(g + 1) * S5_ROW] = (
                    rdt_ref[g, (t0 + k) * S5_GROUP:(t0 + k + 1) * S5_GROUP, :])
        r0 = (S5_CHUNK - 1 - t0) * 128
        wt = jnp.concatenate([tab_ref[r0:r0 + S5_XW, :], tab_ref[r0 - 128:r0 - 128 + S5_XW, :]], axis=1)
        y2 = jnp.dot(x16_ref[...], wt, preferred_element_type=F32)
        y2 = y2 + lax.dot_general(sall_ref[...], rdexp_ref[...], (((1,), (1,)), ((), ())),
                                  preferred_element_type=F32)
        y_ref[t0] = y2[:, 0:128].astype(BF16)
        y_ref[t0 + 1] = y2[:, 128:256].astype(BF16)


def _s5_scan(u_t, ops, layer, q, s0, *, interleave):
    c_rows = u_t.shape[1]
    ls = c_rows // 8
    nb = S5_GROUPS // S5_LB
    x_specs = [pl.BlockSpec((None, c_rows, 128), lambda b, t=t: (t, 0, b)) for t in range(S5_CHUNK)]
    lblk = lambda *shape: pl.BlockSpec((None, S5_LB) + shape, lambda b: (layer, b) + (0,) * len(shape))
    gblk = lambda *shape: pl.BlockSpec((S5_LB,) + shape, lambda b: (b,) + (0,) * len(shape))
    return pl.pallas_call(
        functools.partial(_s5_kernel, ls=ls, interleave=interleave),
        grid=(nb,),
        in_specs=x_specs + [lblk(S5_ROW, S5_ROW), lblk(S5_ROW, S5_ROW),
                            pl.BlockSpec((None, None, S5_NLAG * 128, S5_GROUP), lambda b: (layer, b, 0, 0)),
                            lblk(8, 128), gblk(8, 128)],
        out_specs=[pl.BlockSpec((S5_CHUNK, c_rows, 128), lambda b: (0, 0, b)), gblk(8, 128)],
        out_shape=[jax.ShapeDtypeStruct((S5_CHUNK, c_rows, S5_WIDTH), BF16),
                   jax.ShapeDtypeStruct((S5_GROUPS, 8, 128), F32)],
        scratch_shapes=[pltpu.VMEM((c_rows, S5_XW), BF16),
                        pltpu.VMEM((S5_LB, ls, 8, S5_ROW), F32),
                        pltpu.VMEM((c_rows, S5_LB * S5_ROW), BF16),
                        pltpu.VMEM((S5_XW, S5_ROW), BF16),
                        pltpu.VMEM((S5_ROW, S5_LB * S5_ROW), BF16),
                        pltpu.VMEM((S5_NLAG * 128, 128), BF16)],
        compiler_params=_cparams("arbitrary"),
        name="s5_scan",
    )(*([u_t] * S5_CHUNK), ops['wv'], ops['rdt'], ops['kc'], q, s0)


def _s5_lag_kernel(c_ref, w_ref, o_ref):
    for g in range(S5_LB):
        o_ref[g] = jnp.dot(c_ref[g], w_ref[g], preferred_element_type=F32, precision=HIGHEST)


def _s5_lag_kernels(cc, ww):
    b = cc.shape[0]
    blk = lambda *shape: pl.BlockSpec((S5_LB,) + shape, lambda i: (i,) + (0,) * len(shape))
    return pl.pallas_call(
        _s5_lag_kernel,
        grid=(b // S5_LB,),
        in_specs=[blk(S5_GROUP, 2 * S5_STATE), blk(2 * S5_STATE, S5_ROW)],
        out_specs=blk(S5_GROUP, S5_ROW),
        out_shape=jax.ShapeDtypeStruct((b, S5_GROUP, S5_ROW), F32),
        compiler_params=_cparams("parallel"),
        name="s5_lag",
    )(cc, ww)


def _s5_operators(lam_re, lam_im, log_step, b_re, b_im, c_re, c_im, d, seg_lens):
    t = S5_CHUNK
    depth = lam_re.shape[0]
    lre = jnp.minimum(lam_re.astype(F32), -1e-4)
    lim = lam_im.astype(F32)
    step = jnp.exp(log_step.astype(F32))[..., None]

    def apow(k):
        mag = jnp.exp(k * lre * step)
        return mag * jnp.cos(k * lim * step), mag * jnp.sin(k * lim * step)

    ab_re, ab_im = apow(1.0)
    den = jnp.square(lre) + jnp.square(lim)
    q_re = ((ab_re - 1.0) * lre + ab_im * lim) / den
    q_im = (ab_im * lre - (ab_re - 1.0) * lim) / den
    bre = b_re.astype(F32)[:, None]
    bim = b_im.astype(F32)[:, None]
    bb_re = q_re[..., None] * bre - q_im[..., None] * bim
    bb_im = q_re[..., None] * bim + q_im[..., None] * bre

    ks = jnp.arange(t + 1, dtype=F32)[:, None, None, None, None]
    pw_re, pw_im = apow(ks)
    pw_re = jnp.moveaxis(pw_re, 0, 3)
    pw_im = jnp.moveaxis(pw_im, 0, 3)
    w_re = pw_re[..., None] * bb_re[:, :, :, None] - pw_im[..., None] * bb_im[:, :, :, None]
    w_im = pw_re[..., None] * bb_im[:, :, :, None] + pw_im[..., None] * bb_re[:, :, :, None]
    cre = c_re.astype(F32)
    cim = c_im.astype(F32)

    cc = jnp.concatenate([cre, -cim], axis=-1).reshape(-1, S5_GROUP, 2 * S5_STATE)
    ww = jnp.concatenate([w_re[:, :, :, :t], w_im[:, :, :, :t]], axis=4)
    ww = ww.transpose(0, 1, 2, 4, 3, 5).reshape(-1, 2 * S5_STATE, S5_ROW)
    kk = _s5_lag_kernels(cc, ww).reshape(depth, 2, S5_GROUPS, S5_GROUP, t, S5_GROUP)
    kf, kb = kk[:, 0], kk[:, 1]
    dd = d.astype(F32).reshape(depth, S5_GROUPS, S5_GROUP)
    mid = kf[:, :, :, 0] + kb[:, :, :, 0] + dd[:, :, :, None] * jnp.eye(S5_GROUP, dtype=F32)
    seq = jnp.concatenate([kf[:, :, :, t - 1:0:-1], mid[:, :, :, None], kb[:, :, :, 1:t]], axis=3)
    seq = seq.transpose(0, 1, 3, 4, 2).reshape(depth, S5_GROUPS // S5_LB, S5_LB, S5_NLAG, S5_GROUP, S5_GROUP)
    seq = seq.transpose(0, 1, 3, 2, 4, 5)
    kc = seq.reshape(depth, S5_GROUPS // S5_LB, S5_NLAG * 128, S5_GROUP).astype(BF16)

    def sm(arr, dr, rev):
        a = arr[:, dr, :, :t]
        a = a[:, :, ::-1] if rev else a
        return a.transpose(0, 1, 2, 4, 3).reshape(depth, S5_GROUPS, S5_ROW, S5_STATE)

    wv = jnp.concatenate([sm(w_re, 0, True), sm(w_re, 1, False), sm(w_im, 0, True), sm(w_im, 1, False)],
                         axis=3).astype(BF16)

    g_re = cre[:, :, :, None] * pw_re[:, :, :, :, None] - cim[:, :, :, None] * pw_im[:, :, :, :, None]
    g_im = cre[:, :, :, None] * pw_im[:, :, :, :, None] + cim[:, :, :, None] * pw_re[:, :, :, :, None]

    def rd(arr, dr, rev):
        a = arr[:, dr, :, 1:t + 1]
        a = a[:, :, ::-1] if rev else a
        return a.reshape(depth, S5_GROUPS, S5_ROW, S5_STATE)

    rdt = jnp.concatenate([rd(g_re, 0, False), rd(g_re, 1, True), -rd(g_im, 0, False), -rd(g_im, 1, True)],
                          axis=3).astype(BF16)

    def lanes(re_im):
        return jnp.concatenate([re_im[:, 0], re_im[:, 1]], axis=-1)[:, :, None, :]

    c_r, c_i = apow(float(t))
    qs = []
    for ls in seg_lens:
        s_r, s_i = apow(float(t * ls))
        qs.append(jnp.concatenate([lanes(c_r), lanes(c_i), lanes(s_r), lanes(s_i),
                                   jnp.zeros((depth, S5_GROUPS, 4, 128), F32)], axis=2))
    return {'wv': wv, 'rdt': rdt, 'kc': kc}, qs


def _s5_segment_rows(y, inverse):
    t, c_rows, w = y.shape
    ls = c_rows // 8
    if inverse:
        return y.reshape(t, ls, 8, w).transpose(0, 2, 1, 3).reshape(t, c_rows, w)
    return y.reshape(t, 8, ls, w).transpose(0, 2, 1, 3).reshape(t, c_rows, w)


def _glu_kernel(y_ref, wa_ref, wb_ref, o_ref):
    y = jnp.stack([y_ref[t] for t in range(S5_CHUNK)], axis=1).reshape(o_ref.shape[0], y_ref.shape[2])
    gl = _gelu_tanh(y.astype(F32)).astype(BF16)
    a = jnp.dot(gl, wa_ref[...], preferred_element_type=F32)
    b = jnp.dot(gl, wb_ref[...], preferred_element_type=F32)
    o_ref[...] = (a * _sigmoid(b)).astype(BF16)


def _s5_glu(y_t, w_glu, layer):
    _, c_rows, k = y_t.shape
    n = c_rows * S5_CHUNK
    tm = min(1024, n)
    return pl.pallas_call(
        _glu_kernel,
        grid=(n // tm,),
        in_specs=[pl.BlockSpec((S5_CHUNK, tm // S5_CHUNK, k), lambda i: (0, i, 0)),
                  pl.BlockSpec((None, k, S5_WIDTH), lambda i: (layer, 0, 0)),
                  pl.BlockSpec((None, k, S5_WIDTH), lambda i: (layer, 0, 1))],
        out_specs=pl.BlockSpec((tm, S5_WIDTH), lambda i: (i, 0)),
        out_shape=jax.ShapeDtypeStruct((n, S5_WIDTH), BF16),
        compiler_params=_cparams("parallel"),
        name="s5_glu",
    )(y_t, w_glu, w_glu)


def _ssdconv_kernel(x_ref, xp_ref, xn_ref, w_ref, b_ref, o_ref, *, nt):
    i = pl.program_id(0)
    x = x_ref[...].astype(F32)
    tr = x.shape[0]
    prev = jnp.where(i > 0, xp_ref[...].astype(F32)[15:16, :], 0.0)
    nxt = jnp.where(i < nt - 1, xn_ref[...].astype(F32)[0:1, :], 0.0)
    ri = lax.broadcasted_iota(jnp.int32, x.shape, 0)
    x_m1 = jnp.where(ri == 0, prev, pltpu.roll(x, 1, 0))
    x_p1 = jnp.where(ri == tr - 1, nxt, pltpu.roll(x, tr - 1, 0))
    y = w_ref[0:1, :] * x_m1 + w_ref[1:2, :] * x + w_ref[2:3, :] * x_p1 + b_ref[...]
    o_ref[...] = _silu(y).astype(BF16)


def _ssd_conv(pm, conv_w, conv_b):
    n = pm.shape[0]
    tr = min(512, n)
    tc = 1024
    nt = n // tr
    hb = tr // 16
    return pl.pallas_call(
        functools.partial(_ssdconv_kernel, nt=nt),
        grid=(nt, SSD_CONV_CH // tc),
        in_specs=[pl.BlockSpec((tr, tc), lambda i, j: (i, j)),
                  pl.BlockSpec((16, tc), lambda i, j: (jnp.maximum(i * hb - 1, 0), j)),
                  pl.BlockSpec((16, tc), lambda i, j: (jnp.minimum((i + 1) * hb, n // 16 - 1), j)),
                  pl.BlockSpec((3, tc), lambda i, j: (0, j)),
                  pl.BlockSpec((1, tc), lambda i, j: (0, j))],
        out_specs=pl.BlockSpec((tr, tc), lambda i, j: (i, j)),
        out_shape=jax.ShapeDtypeStruct((n, SSD_CONV_CH), BF16),
        compiler_params=_cparams("parallel", "parallel"),
        name="ssd_conv",
    )(pm, pm, pm, conv_w, conv_b)


def _head_expand(v, mode):
    nh = 2 * SSD_HEADS
    rows = lax.broadcasted_iota(jnp.int32, (2 * nh, SSD_WIDTH), 0) & (nh - 1)
    head = lax.broadcasted_iota(jnp.int32, (2 * nh, SSD_WIDTH), 1) >> 6
    if mode == 'fwd':
        sel = rows == head
    elif mode == 'bwd':
        sel = rows == head + SSD_HEADS
    else:
        sel = (rows & (SSD_HEADS - 1)) == head
    onehot = jnp.where(sel, 1.0, 0.0).astype(BF16)
    hi = v.astype(BF16)
    lo = (v - hi.astype(F32)).astype(BF16)
    return jnp.dot(jnp.concatenate([hi, lo], axis=1), onehot, preferred_element_type=F32)


def _tri_masks(t):
    r = lax.broadcasted_iota(jnp.int32, (t, t), 0)
    c = lax.broadcasted_iota(jnp.int32, (t, t), 1)
    return r, c


SSD_STEP = 2 * SSD_CHUNK


def _ssd_fwd_kernel(xc_ref, dt_ref, dtt_ref, bias_ref, biast_ref, a_ref, at_ref, s0_ref,
                    y_ref, sfin_ref, s_ref, *, nc):
    t = SSD_CHUNK
    c = pl.program_id(0)

    @pl.when(c == 0)
    def _():
        s_ref[...] = s0_ref[...]

    r, cc = _tri_masks(t)
    lo_incl = jnp.where(r >= cc, 1.0, 0.0)
    up_incl = jnp.where(r <= cc, 1.0, 0.0)
    lo = r > cc
    diag = r == cc
    lane = lax.broadcasted_iota(jnp.int32, (t, 128), 1)

    for sub in range(xc_ref.shape[0] // t):
        rs = slice(sub * t, (sub + 1) * t)
        dt = _softplus(dt_ref[rs, 0:2 * SSD_HEADS] + bias_ref[...])
        dtt = _softplus(dtt_ref[:, rs] + biast_ref[...])
        la = dt * a_ref[...]
        lat = dtt * at_ref[...]
        cum_lo = jnp.dot(lo_incl, la, preferred_element_type=F32, precision=HIGHEST)
        cum_up = jnp.dot(up_incl, la, preferred_element_type=F32, precision=HIGHEST)
        cumt_f = jnp.dot(lat, up_incl, preferred_element_type=F32, precision=HIGHEST)
        cumt_b = jnp.dot(lat, lo_incl, preferred_element_type=F32, precision=HIGHEST)

        end_f = cum_lo[t - 1:t, :]
        exp_f = _head_expand(jnp.exp(cum_lo), 'fwd')
        w_f = _head_expand(jnp.exp(end_f - cum_lo) * dt, 'fwd')

        for g in range(SSD_GROUPS):
            gs = slice(g * SSD_GW, (g + 1) * SSD_GW)
            bg = xc_ref[rs, SSD_WIDTH + g * SSD_STATE:SSD_WIDTH + (g + 1) * SSD_STATE]
            cg = xc_ref[rs, SSD_WIDTH + (SSD_GROUPS + g) * SSD_STATE:SSD_WIDTH + (SSD_GROUPS + g + 1) * SSD_STATE]
            cb = lax.dot_general(cg, bg, (((1,), (1,)), ((), ())), preferred_element_type=F32)
            cb = jnp.where(diag, 0.0, cb)
            bt = bg.astype(F32).T.astype(BF16)
            xg = xc_ref[rs, gs]
            sg = s_ref[g]
            y_int = exp_f[:, gs] * jnp.dot(cg, sg.astype(BF16), preferred_element_type=F32)
            xw = (xg.astype(F32) * w_f[:, gs]).astype(BF16)
            s_ref[g] = sg * exp_f[t - 1:t, gs] + jnp.dot(bt, xw, preferred_element_type=F32)
            for p in range(SSD_GW // 128):
                xp = xg[:, p * 128:(p + 1) * 128]
                acc = y_int[:, p * 128:(p + 1) * 128]
                for q in range(2):
                    hh = g * (SSD_HEADS // SSD_GROUPS) + 2 * p + q
                    hb = SSD_HEADS + hh
                    csel = jnp.where(lo, cum_lo[:, hh:hh + 1], cum_up[:, hb:hb + 1])
                    rsel = jnp.where(lo, cumt_f[hh:hh + 1, :], cumt_b[hb:hb + 1, :])
                    dsel = jnp.where(lo, dtt[hh:hh + 1, :], dtt[hb:hb + 1, :])
                    sc = (cb * jnp.exp(csel - rsel) * dsel).astype(BF16)
                    xh = jnp.where((lane < 64) if q == 0 else (lane >= 64), xp, jnp.zeros_like(xp))
                    acc = acc + jnp.dot(sc, xh, preferred_element_type=F32)
                y_ref[rs, g * SSD_GW + p * 128:g * SSD_GW + (p + 1) * 128] = acc

    @pl.when(c == nc - 1)
    def _():
        sfin_ref[...] = s_ref[...]


def _ssd_bwd_kernel(xc_ref, dt_ref, z_ref, yp_ref, bias_ref, a_ref, dvec_ref, nw_ref, s0_ref,
                    o_ref, sfin_ref, s_ref, *, nc):
    t = SSD_CHUNK
    c = pl.program_id(0)

    @pl.when(c == 0)
    def _():
        s_ref[...] = s0_ref[...]

    ts = t
    r, cc = _tri_masks(ts)
    up_incl = jnp.where(r <= cc, 1.0, 0.0)
    grp = (lax.broadcasted_iota(jnp.int32, (ts, 2 * SSD_HEADS), 1) & (SSD_HEADS - 1)) >> 3

    for sub in range(xc_ref.shape[0] // ts - 1, -1, -1):
        rs = slice(sub * ts, (sub + 1) * ts)
        dt = _softplus(dt_ref[rs, 0:2 * SSD_HEADS] + bias_ref[...])
        la = dt * a_ref[...]
        cum_up = jnp.dot(up_incl, la, preferred_element_type=F32, precision=HIGHEST)
        exp_b = _head_expand(jnp.exp(cum_up), 'bwd')
        w_b = _head_expand(jnp.exp(cum_up[0:1, :] - cum_up) * dt, 'bwd')

        bgs = [xc_ref[rs, SSD_WIDTH + g * SSD_STATE:SSD_WIDTH + (g + 1) * SSD_STATE] for g in range(SSD_GROUPS)]
        cgs = [xc_ref[rs, SSD_WIDTH + (SSD_GROUPS + g) * SSD_STATE:SSD_WIDTH + (SSD_GROUPS + g + 1) * SSD_STATE]
               for g in range(SSD_GROUPS)]
        cbd = jnp.zeros((ts, 2 * SSD_HEADS), F32)
        for g in range(SSD_GROUPS):
            dg = jnp.sum(cgs[g].astype(F32) * bgs[g].astype(F32), axis=-1, keepdims=True)
            cbd = jnp.where(grp == g, dg, cbd)
        coef = _head_expand(dt * cbd, 'both') + dvec_ref[...]

        for g in range(SSD_GROUPS):
            gs = slice(g * SSD_GW, (g + 1) * SSD_GW)
            bt = bgs[g].astype(F32).T.astype(BF16)
            xg = xc_ref[rs, gs].astype(F32)
            sg = s_ref[g]
            y_int = exp_b[:, gs] * jnp.dot(cgs[g], sg.astype(BF16), preferred_element_type=F32)
            xw = (xg * w_b[:, gs]).astype(BF16)
            s_ref[g] = sg * exp_b[0:1, gs] + jnp.dot(bt, xw, preferred_element_type=F32)
            y = yp_ref[rs, gs] + y_int + xg * coef[:, gs]
            hcur = y * _silu(z_ref[rs, gs].astype(F32))
            ms = jnp.mean(hcur * hcur, axis=-1, keepdims=True)
            o_ref[rs, gs] = (hcur * lax.rsqrt(ms + RMS_EPS) * nw_ref[:, gs]).astype(BF16)

    @pl.when(c == nc - 1)
    def _():
        sfin_ref[...] = s_ref[...]


def _ssd_scan(xc, dt_raw, dt_t, pm, prm, s0_f, s0_b):
    n = xc.shape[0]
    t = min(SSD_STEP, n)
    nc = n // t
    full = lambda a: pl.BlockSpec(a.shape, lambda c: (0,) * a.ndim)
    s_shape = (SSD_GROUPS, SSD_STATE, SSD_GW)
    s_spec = pl.BlockSpec(s_shape, lambda c: (0, 0, 0))
    y_part, sfin_f = pl.pallas_call(
        functools.partial(_ssd_fwd_kernel, nc=nc),
        grid=(nc,),
        in_specs=[pl.BlockSpec((t, SSD_CONV_CH), lambda c: (c, 0)),
                  pl.BlockSpec((t, DT_PAD), lambda c: (c, 0)),
                  pl.BlockSpec((2 * SSD_HEADS, t), lambda c: (0, c)),
                  full(prm['bias']), full(prm['bias_t']), full(prm['a']), full(prm['a_t']), s_spec],
        out_specs=[pl.BlockSpec((t, SSD_WIDTH), lambda c: (c, 0)), s_spec],
        out_shape=[jax.ShapeDtypeStruct((n, SSD_WIDTH), F32), jax.ShapeDtypeStruct(s_shape, F32)],
        scratch_shapes=[pltpu.VMEM(s_shape, F32)],
        compiler_params=_cparams("arbitrary"),
        name="ssd_fwd",
    )(xc, dt_raw, dt_t, prm['bias'], prm['bias_t'], prm['a'], prm['a_t'], s0_f)
    rev = lambda c: (nc - 1 - c, 0)
    z_blk = PM_Z // SSD_WIDTH
    yd, sfin_b = pl.pallas_call(
        functools.partial(_ssd_bwd_kernel, nc=nc),
        grid=(nc,),
        in_specs=[pl.BlockSpec((t, SSD_CONV_CH), rev),
                  pl.BlockSpec((t, DT_PAD), rev),
                  pl.BlockSpec((t, SSD_WIDTH), lambda c: (nc - 1 - c, z_blk)),
                  pl.BlockSpec((t, SSD_WIDTH), rev),
                  full(prm['bias']), full(prm['a']), full(prm['dvec']), full(prm['norm_w']), s_spec],
        out_specs=[pl.BlockSpec((t, SSD_WIDTH), rev), s_spec],
        out_shape=[jax.ShapeDtypeStruct((n, SSD_WIDTH), BF16), jax.ShapeDtypeStruct(s_shape, F32)],
        scratch_shapes=[pltpu.VMEM(s_shape, F32)],
        compiler_params=_cparams("arbitrary"),
        name="ssd_bwd",
    )(xc, dt_raw, pm, y_part, prm['bias'], prm['a'], prm['dvec'], prm['norm_w'], s0_b)
    return yd, sfin_f, sfin_b


def _merge_kernel(y5_ref, yd_ref, w5_ref, wd_ref, g5_ref, gd_ref, o_ref):
    a = jnp.dot(y5_ref[...], w5_ref[...], preferred_element_type=F32)
    b = jnp.dot(yd_ref[...], wd_ref[...], preferred_element_type=F32)
    o = _sigmoid(g5_ref[...].astype(F32)) * a + _sigmoid(gd_ref[...].astype(F32)) * b
    o_ref[...] = o.astype(BF16)


def _merge(y5, yd, w5, wd, pm, layer):
    n = y5.shape[0]
    tm = min(1024, n)
    tn = 1024
    g5_blk, gd_blk = PM_G5 // tn, PM_GD // tn
    return pl.pallas_call(
        _merge_kernel,
        grid=(n // tm, D_MODEL // tn),
        in_specs=[pl.BlockSpec((tm, S5_WIDTH), lambda i, j: (i, 0)),
                  pl.BlockSpec((tm, SSD_WIDTH), lambda i, j: (i, 0)),
                  pl.BlockSpec((None, S5_WIDTH, tn), lambda i, j: (layer, 0, j)),
                  pl.BlockSpec((None, SSD_WIDTH, tn), lambda i, j: (layer, 0, j)),
                  pl.BlockSpec((tm, tn), lambda i, j: (i, g5_blk + j)),
                  pl.BlockSpec((tm, tn), lambda i, j: (i, gd_blk + j))],
        out_specs=pl.BlockSpec((tm, tn), lambda i, j: (i, j)),
        out_shape=jax.ShapeDtypeStruct((n, D_MODEL), BF16),
        compiler_params=_cparams("parallel", "parallel"),
        name="merge",
    )(y5, yd, w5, wd, pm, pm)


def _proj_ln_kernel(x_ref, w_ref, h_ref, m_ref, lg_ref, lb_ref, o_ref, acc_ref, *, row, g_off, nk, alpha):
    k = pl.program_id(1)
    part = jnp.dot(x_ref[...], w_ref[...], preferred_element_type=F32)

    @pl.when(k == 0)
    def _():
        acc_ref[...] = part

    @pl.when(k > 0)
    def _():
        acc_ref[...] += part

    @pl.when(k == nk - 1)
    def _():
        d = h_ref.shape[1]
        gate = m_ref[row:row + 1, g_off:g_off + d]
        o_ref[...] = _layer_norm(alpha * h_ref[...] + gate * acc_ref[...], lg_ref[...], lb_ref[...])


def _proj_ln(x, w, layer, h, m, ln_g, ln_b, *, row, g_off, tk, alpha):
    n, kdim = x.shape
    d = h.shape[1]
    tm = min(512, n)
    nk = kdim // tk
    return pl.pallas_call(
        functools.partial(_proj_ln_kernel, row=row, g_off=g_off, nk=nk, alpha=alpha),
        grid=(n // tm, nk),
        in_specs=[pl.BlockSpec((tm, tk), lambda i, k: (i, k)),
                  pl.BlockSpec((None, tk, d), lambda i, k: (layer, k, 0)),
                  pl.BlockSpec((tm, d), lambda i, k: (i, 0)),
                  pl.BlockSpec(m.shape, lambda i, k: (0, 0)),
                  pl.BlockSpec((1, d), lambda i, k: (0, 0)),
                  pl.BlockSpec((1, d), lambda i, k: (0, 0))],
        out_specs=pl.BlockSpec((tm, d), lambda i, k: (i, 0)),
        out_shape=jax.ShapeDtypeStruct((n, d), F32),
        scratch_shapes=[pltpu.VMEM((tm, d), F32)],
        compiler_params=_cparams("parallel", "arbitrary"),
        name="proj_ln",
    )(x, w, h, m, ln_g, ln_b)


FFN_HALO = 128


def _ffnconv_kernel(g_ref, gp_ref, gn_ref, v_ref, w_ref, b_ref, o_ref, ext_ref, *, nt, grid_mode):
    i = pl.program_id(0)
    tr = g_ref.shape[0]
    hl = FFN_HALO
    ext_ref[0:hl, :] = jnp.where(i > 0, gp_ref[...].astype(F32), 0.0)
    ext_ref[hl:hl + tr, :] = g_ref[...].astype(F32)
    ext_ref[hl + tr:hl + tr + hl, :] = jnp.where(i < nt - 1, gn_ref[...].astype(F32), 0.0)

    def tap(dy, dx):
        wrow = (dy + 1) * 3 + (dx + 1)
        return w_ref[wrow:wrow + 1, :] * ext_ref[pl.ds(hl + GRID_W * dy + dx, tr), :]

    if grid_mode:
        col = lax.broadcasted_iota(jnp.int32, (tr, g_ref.shape[1]), 0) & (GRID_W - 1)
        acc_c = tap(-1, 0) + tap(0, 0) + tap(1, 0)
        acc_l = tap(-1, -1) + tap(0, -1) + tap(1, -1)
        acc_r = tap(-1, 1) + tap(0, 1) + tap(1, 1)
        y = acc_c + jnp.where(col != 0, acc_l, 0.0) + jnp.where(col != GRID_W - 1, acc_r, 0.0)
    else:
        y = tap(0, -1) + tap(0, 0) + tap(0, 1)
    y = y + b_ref[...]
    o_ref[...] = (_silu(y) * v_ref[...].astype(F32)).astype(BF16)


def _ffn_conv(up, conv_w9, conv_b, *, grid_mode):
    n = up.shape[0]
    tr = min(1024, n)
    tc = 512
    nt = n // tr
    hb = tr // FFN_HALO
    ncb = D_FF // tc
    return pl.pallas_call(
        functools.partial(_ffnconv_kernel, nt=nt, grid_mode=grid_mode),
        grid=(nt, ncb),
        in_specs=[pl.BlockSpec((tr, tc), lambda i, j: (i, j)),
                  pl.BlockSpec((FFN_HALO, tc), lambda i, j: (jnp.maximum(i * hb - 1, 0), j)),
                  pl.BlockSpec((FFN_HALO, tc), lambda i, j: (jnp.minimum((i + 1) * hb, n // FFN_HALO - 1), j)),
                  pl.BlockSpec((tr, tc), lambda i, j: (i, ncb + j)),
                  pl.BlockSpec((9, tc), lambda i, j: (0, j)),
                  pl.BlockSpec((1, tc), lambda i, j: (0, j))],
        out_specs=pl.BlockSpec((tr, tc), lambda i, j: (i, j)),
        out_shape=jax.ShapeDtypeStruct((n, D_FF), BF16),
        scratch_shapes=[pltpu.VMEM((tr + 2 * FFN_HALO, tc), F32)],
        compiler_params=_cparams("parallel", "parallel"),
        name="ffn_conv",
    )(up, up, up, up, conv_w9, conv_b)


def _mixer(h, m, row, sw, lw, layer, q, s5_init, ssd_init):
    pm, dt_raw, u_t = _modmm(h, m, sw['w_main'], layer, row=row, sh_off=0, wdt=sw['w_dt'])
    interleave = u_t.shape[1] % 64 == 0
    if not interleave:
        u_t = _s5_segment_rows(u_t, False)
    y5t, s5_fin = _s5_scan(u_t, sw['s5'], layer, q, s5_init, interleave=interleave)
    if not interleave:
        y5t = _s5_segment_rows(y5t, True)
    y5 = _s5_glu(y5t, sw['s5_w_glu'], layer)
    xc = _ssd_conv(pm, lw['ssd_conv_w'], lw['ssd_conv_b'])
    dt_t = dt_raw[:, 0:2 * SSD_HEADS].T
    yd, sf, sb = _ssd_scan(xc, dt_raw, dt_t, pm, lw['ssd'], ssd_init[0], ssd_init[1])
    return pm, y5, yd, s5_fin, (sf, sb)


def _rest_of_layer(h, m, row, sw, lw, layer, pm, y5, yd, alpha, *, grid_mode):
    d = D_MODEL
    merged = _merge(y5, yd, sw['s5_w_proj'], sw['ssd_w_proj'], pm, layer)
    h1 = _proj_ln(merged, sw['w_out'], layer, h, m, lw['ln1_g'], lw['ln1_b'], row=row, g_off=2 * d, tk=d, alpha=alpha)
    up = _modmm(h1, m, sw['w_up'], layer, row=row, sh_off=3 * d)
    act = _ffn_conv(up, lw['ffn_conv_w'], lw['ffn_conv_b'], grid_mode=grid_mode)
    return _proj_ln(act, sw['w_down'], layer, h1, m, lw['ln2_g'], lw['ln2_b'], row=row, g_off=5 * d, tk=D_FF // 4,
                    alpha=alpha)


def kernel(x, c, ctx, c_ctx, w_ada, b_ada, w_in, s5_lam_re, s5_lam_im, s5_log_step, s5_b_re, s5_b_im, s5_c_re, s5_c_im, s5_d, s5_w_glu, s5_w_proj, ssd_conv_w, ssd_conv_b, ssd_dt_bias, ssd_a_log, ssd_d, ssd_norm_w, ssd_w_proj, w_out, ln1_g, ln1_b, w_up, ffn_conv_w, ffn_conv_b, w_down, ln2_g, ln2_b):
    depth = w_ada.shape[0]
    alpha = float((2 * depth) ** 0.25)
    n_lat, n_ctx = x.shape[1], ctx.shape[1]
    h_lat, h_ctx = x[0], ctx[0]

    cc = jnp.zeros((8, D_MODEL), F32).at[0].set(c[0]).at[1].set(c_ctx)
    mods = _adaln(cc, w_ada, b_ada)

    seg = (n_lat // (8 * S5_CHUNK), n_ctx // (8 * S5_CHUNK))
    s5_ops, (q_lat, q_ctx) = _s5_operators(s5_lam_re, s5_lam_im, s5_log_step, s5_b_re, s5_b_im,
                                           s5_c_re, s5_c_im, s5_d, seg)

    sw = {
        'w_main': _repack_w_in(w_in),
        'w_dt': jnp.pad(w_in[:, :, COL_DT:COL_Z], ((0, 0), (0, 0), (0, DT_PAD - 2 * SSD_HEADS))).astype(BF16),
        's5': s5_ops, 's5_w_glu': s5_w_glu.astype(BF16), 's5_w_proj': s5_w_proj.astype(BF16),
        'ssd_w_proj': ssd_w_proj.astype(BF16), 'w_out': w_out.astype(BF16),
        'w_up': w_up.astype(BF16), 'w_down': w_down.astype(BF16),
    }
    a64 = (-jnp.exp(ssd_a_log.astype(F32))).reshape(depth, 1, 2 * SSD_HEADS)
    bias64 = ssd_dt_bias.astype(F32).reshape(depth, 1, 2 * SSD_HEADS)
    dvec = jnp.repeat(ssd_d.astype(F32), SSD_HEAD_DIM, axis=1).reshape(depth, 1, SSD_WIDTH)

    s5_zero = jnp.zeros((S5_GROUPS, 8, 128), F32)
    ssd_zero = jnp.zeros((SSD_GROUPS, SSD_STATE, SSD_GW), F32)

    for i in range(depth):
        last = i == depth - 1
        lw = {
            'ssd_conv_w': ssd_conv_w[i], 'ssd_conv_b': ssd_conv_b[i].reshape(1, SSD_CONV_CH),
            'ssd': {'bias': bias64[i], 'bias_t': bias64[i].reshape(2 * SSD_HEADS, 1),
                    'a': a64[i], 'a_t': a64[i].reshape(2 * SSD_HEADS, 1),
                    'dvec': dvec[i], 'norm_w': ssd_norm_w[i].astype(F32).reshape(1, SSD_WIDTH)},
            'ln1_g': ln1_g[i].reshape(1, D_MODEL), 'ln1_b': ln1_b[i].reshape(1, D_MODEL),
            'ffn_conv_w': ffn_conv_w[i].reshape(9, D_FF), 'ffn_conv_b': ffn_conv_b[i].reshape(1, D_FF),
            'ln2_g': ln2_g[i].reshape(1, D_MODEL), 'ln2_b': ln2_b[i].reshape(1, D_MODEL),
        }
        m = mods[i]
        pm_c, y5_c, yd_c, s5_fin, ssd_fin = _mixer(h_ctx, m, 1, sw, lw, i, q_ctx, s5_zero, (ssd_zero, ssd_zero))
        pm_l, y5_l, yd_l, _, _ = _mixer(h_lat, m, 0, sw, lw, i, q_lat, s5_fin, ssd_fin)
        h_lat = _rest_of_layer(h_lat, m, 0, sw, lw, i, pm_l, y5_l, yd_l, alpha, grid_mode=True)
        if not last:
            h_ctx = _rest_of_layer(h_ctx, m, 1, sw, lw, i, pm_c, y5_c, yd_c, alpha, grid_mode=False)
    return h_lat[None]
```

```python
import functools

import jax
import jax.numpy as jnp
from jax import lax
from jax.experimental import pallas as pl
from jax.experimental.pallas import tpu as pltpu

F32 = jnp.float32
BF16 = jnp.bfloat16
HIGHEST = lax.Precision.HIGHEST

D_MODEL = 2048
GRID_W = 64
S5_WIDTH = 1024
S5_GROUP = 16
S5_GROUPS = S5_WIDTH // S5_GROUP
S5_STATE = 64
S5_CHUNK = 16
S5_ROW = S5_CHUNK * S5_GROUP
S5_LB = 128 // S5_GROUP
S5_XW = S5_CHUNK * 128
S5_NLAG = 2 * S5_CHUNK - 1
SSD_WIDTH = 2048
SSD_HEAD_DIM = 64
SSD_HEADS = SSD_WIDTH // SSD_HEAD_DIM
SSD_GROUPS = 4
SSD_STATE = 128
SSD_CHUNK = 128
SSD_GW = SSD_WIDTH // SSD_GROUPS
SSD_CONV_CH = SSD_WIDTH + 2 * SSD_GROUPS * SSD_STATE
D_FF = 5632
COL_XBC = S5_WIDTH
COL_DT = COL_XBC + SSD_CONV_CH
COL_Z = COL_DT + 2 * SSD_HEADS
COL_GATES = COL_Z + SSD_WIDTH
LN_EPS = 1e-5
RMS_EPS = 1e-5

PM_XBC = 0
PM_U = SSD_CONV_CH
PM_Z = PM_U + S5_WIDTH
PM_G5 = PM_Z + SSD_WIDTH
PM_GD = PM_G5 + D_MODEL
PM_COLS = PM_GD + D_MODEL
DT_PAD = 128

VMEM_LIMIT = 56 * 1024 * 1024


def _cparams(*sem):
    return pltpu.CompilerParams(dimension_semantics=sem, vmem_limit_bytes=VMEM_LIMIT)


def _sigmoid(x):
    return 1.0 / (1.0 + jnp.exp(-x))


def _silu(x):
    return x * _sigmoid(x)


def _gelu_tanh(x):
    return 0.5 * x * (1.0 + jnp.tanh(0.7978845608028654 * (x + 0.044715 * x * x * x)))


def _softplus(x):
    return jnp.maximum(x, 0.0) + jnp.log1p(jnp.exp(-jnp.abs(x)))


def _layer_norm(t, gam, bet):
    mu = jnp.mean(t, axis=-1, keepdims=True)
    d = t - mu
    var = jnp.mean(d * d, axis=-1, keepdims=True)
    return d * lax.rsqrt(var + LN_EPS) * gam + bet


def _adaln_kernel(c_ref, w_ref, b_ref, o_ref):
    s = _silu(c_ref[...]).astype(BF16)
    o_ref[0] = jnp.dot(s, w_ref[0].astype(BF16), preferred_element_type=F32) + b_ref[0]


def _adaln(cc, w_ada, b_ada):
    depth, d, n6 = w_ada.shape
    tn = 1024
    return pl.pallas_call(
        _adaln_kernel,
        grid=(depth, n6 // tn),
        in_specs=[pl.BlockSpec((8, d), lambda l, j: (0, 0)),
                  pl.BlockSpec((1, d, tn), lambda l, j: (l, 0, j)),
                  pl.BlockSpec((1, 1, tn), lambda l, j: (l, 0, j))],
        out_specs=pl.BlockSpec((1, 8, tn), lambda l, j: (l, 0, j)),
        out_shape=jax.ShapeDtypeStruct((depth, 8, n6), F32),
        compiler_params=_cparams("parallel", "parallel"),
        name="adaln",
    )(cc, w_ada, b_ada.reshape(depth, 1, n6))


REPACK_TN = 1024


def _repack_kernel(a_ref, b_ref, o_ref, dt_ref, *, first_shifted):
    j = pl.program_id(1)
    a = a_ref[...]
    sh = 2 * SSD_HEADS
    shifted = jnp.concatenate([a[sh:, :], b_ref[0:sh, :]], axis=0)
    o_ref[...] = jnp.where(j >= first_shifted, shifted, a).T.astype(BF16)

    @pl.when(j == first_shifted)
    def _():
        keep = lax.broadcasted_iota(jnp.int32, (a.shape[1], DT_PAD), 1) < sh
        dt_ref[...] = jnp.where(keep, a[0:DT_PAD, :].T, 0.0).astype(BF16)


def _repack_w_in(w_in):
    depth, d, _ = w_in.shape
    w_t = jnp.swapaxes(w_in, 1, 2)
    tn = REPACK_TN
    n_xbc, n_u = SSD_CONV_CH // tn, S5_WIDTH // tn
    first_shifted = n_xbc + n_u

    def a_idx(l, j):
        src = jnp.where(j < n_xbc, j + COL_XBC // tn, jnp.where(j < first_shifted, j - n_xbc, j))
        return (l, src, 0)

    def b_idx(l, j):
        return (l, jnp.where(j >= first_shifted, (j + 1) * (tn // 128), 0), 0)

    return pl.pallas_call(
        functools.partial(_repack_kernel, first_shifted=first_shifted),
        grid=(depth, PM_COLS // tn),
        in_specs=[pl.BlockSpec((None, tn, d), a_idx), pl.BlockSpec((None, 128, d), b_idx)],
        out_specs=[pl.BlockSpec((None, d, tn), lambda l, j: (l, 0, j)),
                   pl.BlockSpec((None, d, DT_PAD), lambda l, j: (l, 0, 0))],
        out_shape=[jax.ShapeDtypeStruct((depth, d, PM_COLS), BF16), jax.ShapeDtypeStruct((depth, d, DT_PAD), BF16)],
        compiler_params=_cparams("parallel", "arbitrary"),
        name="repack_w_in",
    )(w_t, w_t)


def _modulated(h_ref, m_ref, row, sh_off):
    d = h_ref.shape[1]
    sh = m_ref[row:row + 1, sh_off:sh_off + d]
    sc = m_ref[row:row + 1, sh_off + d:sh_off + 2 * d]
    return (h_ref[...] * (1.0 + sc) + sh).astype(BF16)


def _modmm_kernel(h_ref, m_ref, w_ref, o_ref, xs_ref, *, row, sh_off):
    @pl.when(pl.program_id(1) == 0)
    def _():
        xs_ref[...] = _modulated(h_ref, m_ref, row, sh_off)

    o_ref[...] = jnp.dot(xs_ref[...], w_ref[...], preferred_element_type=F32).astype(o_ref.dtype)


def _modmm_dt_kernel(h_ref, m_ref, w_ref, wdt_ref, o_ref, dt_ref, ut_ref, xs_ref, *, row, sh_off, u_tile):
    j = pl.program_id(1)

    @pl.when(j == 0)
    def _():
        xs = _modulated(h_ref, m_ref, row, sh_off)
        xs_ref[...] = xs
        dt_ref[...] = jnp.dot(xs, wdt_ref[...], preferred_element_type=F32)

    res = jnp.dot(xs_ref[...], w_ref[...], preferred_element_type=F32).astype(BF16)
    o_ref[...] = res

    @pl.when(j == u_tile)
    def _():
        r3 = res.reshape(res.shape[0] // S5_CHUNK, S5_CHUNK, res.shape[1])
        for t in range(S5_CHUNK):
            ut_ref[t] = r3[:, t, :]


def _modmm(h, m, w, layer, *, row, sh_off, wdt=None):
    n, d = h.shape
    ncols = w.shape[2]
    tm = min(1024, n)
    tn = 1024
    grid = (n // tm, ncols // tn)
    h_spec = pl.BlockSpec((tm, d), lambda i, j: (i, 0))
    m_spec = pl.BlockSpec(m.shape, lambda i, j: (0, 0))
    w_spec = pl.BlockSpec((None, d, tn), lambda i, j: (layer, 0, j))
    o_spec = pl.BlockSpec((tm, tn), lambda i, j: (i, j))
    scratch = [pltpu.VMEM((tm, d), BF16)]
    if wdt is None:
        return pl.pallas_call(
            functools.partial(_modmm_kernel, row=row, sh_off=sh_off),
            grid=grid, in_specs=[h_spec, m_spec, w_spec], out_specs=o_spec,
            out_shape=jax.ShapeDtypeStruct((n, ncols), BF16),
            scratch_shapes=scratch, compiler_params=_cparams("parallel", "arbitrary"),
            name="modmm",
        )(h, m, w)
    return pl.pallas_call(
        functools.partial(_modmm_dt_kernel, row=row, sh_off=sh_off, u_tile=PM_U // tn),
        grid=grid,
        in_specs=[h_spec, m_spec, w_spec, pl.BlockSpec((None, d, DT_PAD), lambda i, j: (layer, 0, 0))],
        out_specs=[o_spec, pl.BlockSpec((tm, DT_PAD), lambda i, j: (i, 0)),
                   pl.BlockSpec((S5_CHUNK, tm // S5_CHUNK, S5_WIDTH), lambda i, j: (0, i, 0))],
        out_shape=[jax.ShapeDtypeStruct((n, ncols), BF16), jax.ShapeDtypeStruct((n, DT_PAD), F32),
                   jax.ShapeDtypeStruct((S5_CHUNK, n // S5_CHUNK, S5_WIDTH), BF16)],
        scratch_shapes=scratch, compiler_params=_cparams("parallel", "arbitrary"),
        name="modmm_dt",
    )(h, m, w, wdt)


def _s5_kernel(*refs, ls, interleave):
    x_refs = refs[:S5_CHUNK]
    (wv_ref, rdt_ref, kc_ref, q_ref, s0_ref, y_ref, fin_ref,
     x16_ref, v_ref, sall_ref, wexp_ref, rdexp_ref, tab_ref) = refs[S5_CHUNK:]
    gb = S5_LB
    c_rows = ls * 8
    lane = lax.broadcasted_iota(jnp.int32, (8, 128), 1)
    fwd = lane < 64
    row = lax.broadcasted_iota(jnp.int32, (8, 128), 0)
    zero_tile = jnp.zeros((S5_GROUP, S5_ROW), BF16)

    for t in range(S5_CHUNK):
        x16_ref[:, t * 128:(t + 1) * 128] = x_refs[t][...]

    rep = jnp.where((lax.broadcasted_iota(jnp.int32, (S5_GROUP, 128), 1) & (S5_GROUP - 1))
                    == lax.broadcasted_iota(jnp.int32, (S5_GROUP, 128), 0), 1.0, 0.0).astype(BF16)
    spread = jnp.dot(kc_ref[...], rep, preferred_element_type=F32)
    own = (((lax.broadcasted_iota(jnp.int32, spread.shape, 0) >> 4) & (S5_LB - 1))
           == (lax.broadcasted_iota(jnp.int32, spread.shape, 1) >> 4))
    tab_ref[...] = jnp.where(own, spread, 0.0).astype(BF16)

    wexp_ref[...] = jnp.zeros(wexp_ref.shape, BF16)
    for g in range(gb):
        for t in range(S5_CHUNK):
            r0 = t * 128 + g * S5_GROUP
            wexp_ref[r0:r0 + S5_GROUP, :] = wv_ref[g, t * S5_GROUP:(t + 1) * S5_GROUP, :]
        v = jnp.dot(x16_ref[...], wexp_ref[...], preferred_element_type=F32)
        if interleave:
            v3 = v.reshape(8, ls, S5_ROW)
            v_ref[g] = jnp.stack([v3[s] for s in range(8)], axis=1)
        else:
            v_ref[g] = v.reshape(ls, 8, S5_ROW)
        for t in range(S5_CHUNK):
            r0 = t * 128 + g * S5_GROUP
            wexp_ref[r0:r0 + S5_GROUP, :] = zero_tile

    qr = [q_ref[g, 0:1, :] for g in range(gb)]
    qi = [q_ref[g, 1:2, :] for g in range(gb)]

    def scan_body(i, carry):
        out = []
        for g in range(gb):
            cr, ci = carry[2 * g], carry[2 * g + 1]
            a = v_ref[g, i]
            b = v_ref[g, ls - 1 - i]
            vr = jnp.where(fwd, a[:, 0:128], b[:, 0:128])
            vi = jnp.where(fwd, a[:, 128:256], b[:, 128:256])
            v_ref[g, i, :, 0:64] = cr[:, 0:64]
            v_ref[g, i, :, 128:192] = ci[:, 0:64]
            v_ref[g, ls - 1 - i, :, 64:128] = cr[:, 64:128]
            v_ref[g, ls - 1 - i, :, 192:256] = ci[:, 64:128]
            out.append(qr[g] * cr - qi[g] * ci + vr)
            out.append(qr[g] * ci + qi[g] * cr + vi)
        return tuple(out)

    zero = jnp.zeros((8, 128), F32)
    tot = lax.fori_loop(0, ls, scan_body, tuple(zero for _ in range(2 * gb)))

    cins = []
    for g in range(gb):
        tr, ti = tot[2 * g], tot[2 * g + 1]
        lr, li = q_ref[g, 2:3, :], q_ref[g, 3:4, :]
        s0r, s0i = s0_ref[g, 0:1, :], s0_ref[g, 1:2, :]
        cf = [(s0r, s0i)]
        for s in range(1, 8):
            pr, pi = cf[-1]
            cf.append((lr * pr - li * pi + tr[s - 1:s, :], lr * pi + li * pr + ti[s - 1:s, :]))
        cb = [(s0r, s0i)]
        for s in range(6, -1, -1):
            pr, pi = cb[-1]
            cb.append((lr * pr - li * pi + tr[s + 1:s + 2, :], lr * pi + li * pr + ti[s + 1:s + 2, :]))
        cb = cb[::-1]
        cin_r, cin_i = zero, zero
        for s in range(8):
            sel = row == s
            cin_r = jnp.where(sel, jnp.where(fwd, cf[s][0], cb[s][0]), cin_r)
            cin_i = jnp.where(sel, jnp.where(fwd, cf[s][1], cb[s][1]), cin_i)
        cins += [cin_r, cin_i]
        ff_r = lr * cf[7][0] - li * cf[7][1] + tr[7:8, :]
        ff_i = lr * cf[7][1] + li * cf[7][0] + ti[7:8, :]
        fb_r = lr * cb[0][0] - li * cb[0][1] + tr[0:1, :]
        fb_i = lr * cb[0][1] + li * cb[0][0] + ti[0:1, :]
        fin_ref[g, 0:1, :] = jnp.where(fwd[0:1], ff_r, fb_r)
        fin_ref[g, 1:2, :] = jnp.where(fwd[0:1], ff_i, fb_i)
        fin_ref[g, 2:8, :] = jnp.zeros((6, 128), F32)

    def fix_body(i, carry):
        out = []
        for g in range(gb):
            dr, di = carry[2 * g], carry[2 * g + 1]
            add_f = jnp.concatenate([jnp.where(fwd, dr, 0.0), jnp.where(fwd, di, 0.0)], axis=1)
            v_ref[g, i] = v_ref[g, i] + add_f
            add_b = jnp.concatenate([jnp.where(fwd, 0.0, dr), jnp.where(fwd, 0.0, di)], axis=1)
            v_ref[g, ls - 1 - i] = v_ref[g, ls - 1 - i] + add_b
            out.append(qr[g] * dr - qi[g] * di)
            out.append(qr[g] * di + qi[g] * dr)
        return tuple(out)

    lax.fori_loop(0, ls, fix_body, tuple(cins))

    for g in range(gb):
        sg = v_ref[g]
        if interleave:
            sin = jnp.concatenate([sg[:, s, :] for s in range(8)], axis=0)
        else:
            sin = sg.reshape(c_rows, S5_ROW)
        sall_ref[:, g * S5_ROW:(g + 1) * S5_ROW] = sin.astype(BF16)

    rdexp_ref[...] = jnp.zeros(rdexp_ref.shape, BF16)
    for tp in range(S5_CHUNK // 2):
        t0 = 2 * tp
        for k in range(2):
            for g in range(gb):
                r0 = k * 128 + g * S5_GROUP
                rdexp_ref[r0:r0 + S5_GROUP, g * S5_ROW:(g + 1) * S5_ROW] = (
                    rdt_ref[g, (t0 + k) * S5_GROUP:(t0 + k + 1) * S5_GROUP, :])
        r0 = (S5_CHUNK - 1 - t0) * 128
        wt = jnp.concatenate([tab_ref[r0:r0 + S5_XW, :], tab_ref[r0 - 128:r0 - 128 + S5_XW, :]], axis=1)
        y2 = jnp.dot(x16_ref[...], wt, preferred_element_type=F32)
        y2 = y2 + lax.dot_general(sall_ref[...], rdexp_ref[...], (((1,), (1,)), ((), ())),
                                  preferred_element_type=F32)
        y_ref[t0] = y2[:, 0:128].astype(BF16)
        y_ref[t0 + 1] = y2[:, 128:256].astype(BF16)


def _s5_scan(u_t, ops, layer, q, s0, *, interleave):
    c_rows = u_t.shape[1]
    ls = c_rows // 8
    nb = S5_GROUPS // S5_LB
    x_specs = [pl.BlockSpec((None, c_rows, 128), lambda b, t=t: (t, 0, b)) for t in range(S5_CHUNK)]
    lblk = lambda *shape: pl.BlockSpec((None, S5_LB) + shape, lambda b: (layer, b) + (0,) * len(shape))
    gblk = lambda *shape: pl.BlockSpec((S5_LB,) + shape, lambda b: (b,) + (0,) * len(shape))
    return pl.pallas_call(
        functools.partial(_s5_kernel, ls=ls, interleave=interleave),
        grid=(nb,),
        in_specs=x_specs + [lblk(S5_ROW, S5_ROW), lblk(S5_ROW, S5_ROW),
                            pl.BlockSpec((None, None, S5_NLAG * 128, S5_GROUP), lambda b: (layer, b, 0, 0)),
                            lblk(8, 128), gblk(8, 128)],
        out_specs=[pl.BlockSpec((S5_CHUNK, c_rows, 128), lambda b: (0, 0, b)), gblk(8, 128)],
        out_shape=[jax.ShapeDtypeStruct((S5_CHUNK, c_rows, S5_WIDTH), BF16),
                   jax.ShapeDtypeStruct((S5_GROUPS, 8, 128), F32)],
        scratch_shapes=[pltpu.VMEM((c_rows, S5_XW), BF16),
                        pltpu.VMEM((S5_LB, ls, 8, S5_ROW), F32),
                        pltpu.VMEM((c_rows, S5_LB * S5_ROW), BF16),
                        pltpu.VMEM((S5_XW, S5_ROW), BF16),
                        pltpu.VMEM((S5_ROW, S5_LB * S5_ROW), BF16),
                        pltpu.VMEM((S5_NLAG * 128, 128), BF16)],
        compiler_params=_cparams("arbitrary"),
        name="s5_scan",
    )(*([u_t] * S5_CHUNK), ops['wv'], ops['rdt'], ops['kc'], q, s0)


def _s5_lag_kernel(c_ref, w_ref, o_ref):
    for g in range(S5_LB):
        o_ref[g] = jnp.dot(c_ref[g], w_ref[g], preferred_element_type=F32, precision=HIGHEST)


def _s5_lag_kernels(cc, ww):
    b = cc.shape[0]
    blk = lambda *shape: pl.BlockSpec((S5_LB,) + shape, lambda i: (i,) + (0,) * len(shape))
    return pl.pallas_call(
        _s5_lag_kernel,
        grid=(b // S5_LB,),
        in_specs=[blk(S5_GROUP, 2 * S5_STATE), blk(2 * S5_STATE, S5_ROW)],
        out_specs=blk(S5_GROUP, S5_ROW),
        out_shape=jax.ShapeDtypeStruct((b, S5_GROUP, S5_ROW), F32),
        compiler_params=_cparams("parallel"),
        name="s5_lag",
    )(cc, ww)


def _s5_operators(lam_re, lam_im, log_step, b_re, b_im, c_re, c_im, d, seg_lens):
    t = S5_CHUNK
    depth = lam_re.shape[0]
    lre = jnp.minimum(lam_re.astype(F32), -1e-4)
    lim = lam_im.astype(F32)
    step = jnp.exp(log_step.astype(F32))[..., None]

    def apow(k):
        mag = jnp.exp(k * lre * step)
        return mag * jnp.cos(k * lim * step), mag * jnp.sin(k * lim * step)

    ab_re, ab_im = apow(1.0)
    den = jnp.square(lre) + jnp.square(lim)
    q_re = ((ab_re - 1.0) * lre + ab_im * lim) / den
    q_im = (ab_im * lre - (ab_re - 1.0) * lim) / den
    bre = b_re.astype(F32)[:, None]
    bim = b_im.astype(F32)[:, None]
    bb_re = q_re[..., None] * bre - q_im[..., None] * bim
    bb_im = q_re[..., None] * bim + q_im[..., None] * bre

    ks = jnp.arange(t + 1, dtype=F32)[:, None, None, None, None]
    pw_re, pw_im = apow(ks)
    pw_re = jnp.moveaxis(pw_re, 0, 3)
    pw_im = jnp.moveaxis(pw_im, 0, 3)
    w_re = pw_re[..., None] * bb_re[:, :, :, None] - pw_im[..., None] * bb_im[:, :, :, None]
    w_im = pw_re[..., None] * bb_im[:, :, :, None] + pw_im[..., None] * bb_re[:, :, :, None]
    cre = c_re.astype(F32)
    cim = c_im.astype(F32)

    cc = jnp.concatenate([cre, -cim], axis=-1).reshape(-1, S5_GROUP, 2 * S5_STATE)
    ww = jnp.concatenate([w_re[:, :, :, :t], w_im[:, :, :, :t]], axis=4)
    ww = ww.transpose(0, 1, 2, 4, 3, 5).reshape(-1, 2 * S5_STATE, S5_ROW)
    kk = _s5_lag_kernels(cc, ww).reshape(depth, 2, S5_GROUPS, S5_GROUP, t, S5_GROUP)
    kf, kb = kk[:, 0], kk[:, 1]
    dd = d.astype(F32).reshape(depth, S5_GROUPS, S5_GROUP)
    mid = kf[:, :, :, 0] + kb[:, :, :, 0] + dd[:, :, :, None] * jnp.eye(S5_GROUP, dtype=F32)
    seq = jnp.concatenate([kf[:, :, :, t - 1:0:-1], mid[:, :, :, None], kb[:, :, :, 1:t]], axis=3)
    seq = seq.transpose(0, 1, 3, 4, 2).reshape(depth, S5_GROUPS // S5_LB, S5_LB, S5_NLAG, S5_GROUP, S5_GROUP)
    seq = seq.transpose(0, 1, 3, 2, 4, 5)
    kc = seq.reshape(depth, S5_GROUPS // S5_LB, S5_NLAG * 128, S5_GROUP).astype(BF16)

    def sm(arr, dr, rev):
        a = arr[:, dr, :, :t]
        a = a[:, :, ::-1] if rev else a
        return a.transpose(0, 1, 2, 4, 3).reshape(depth, S5_GROUPS, S5_ROW, S5_STATE)

    wv = jnp.concatenate([sm(w_re, 0, True), sm(w_re, 1, False), sm(w_im, 0, True), sm(w_im, 1, False)],
                         axis=3).astype(BF16)

    g_re = cre[:, :, :, None] * pw_re[:, :, :, :, None] - cim[:, :, :, None] * pw_im[:, :, :, :, None]
    g_im = cre[:, :, :, None] * pw_im[:, :, :, :, None] + cim[:, :, :, None] * pw_re[:, :, :, :, None]

    def rd(arr, dr, rev):
        a = arr[:, dr, :, 1:t + 1]
        a = a[:, :, ::-1] if rev else a
        return a.reshape(depth, S5_GROUPS, S5_ROW, S5_STATE)

    rdt = jnp.concatenate([rd(g_re, 0, False), rd(g_re, 1, True), -rd(g_im, 0, False), -rd(g_im, 1, True)],
                          axis=3).astype(BF16)

    def lanes(re_im):
        return jnp.concatenate([re_im[:, 0], re_im[:, 1]], axis=-1)[:, :, None, :]

    c_r, c_i = apow(float(t))
    qs = []
    for ls in seg_lens:
        s_r, s_i = apow(float(t * ls))
        qs.append(jnp.concatenate([lanes(c_r), lanes(c_i), lanes(s_r), lanes(s_i),
                                   jnp.zeros((depth, S5_GROUPS, 4, 128), F32)], axis=2))
    return {'wv': wv, 'rdt': rdt, 'kc': kc}, qs


def _s5_segment_rows(y, inverse):
    t, c_rows, w = y.shape
    ls = c_rows // 8
    if inverse:
        return y.reshape(t, ls, 8, w).transpose(0, 2, 1, 3).reshape(t, c_rows, w)
    return y.reshape(t, 8, ls, w).transpose(0, 2, 1, 3).reshape(t, c_rows, w)


def _glu_kernel(y_ref, wa_ref, wb_ref, o_ref):
    y = jnp.stack([y_ref[t] for t in range(S5_CHUNK)], axis=1).reshape(o_ref.shape[0], y_ref.shape[2])
    gl = _gelu_tanh(y.astype(F32)).astype(BF16)
    a = jnp.dot(gl, wa_ref[...], preferred_element_type=F32)
    b = jnp.dot(gl, wb_ref[...], preferred_element_type=F32)
    o_ref[...] = (a * _sigmoid(b)).astype(BF16)


def _s5_glu(y_t, w_glu, layer):
    _, c_rows, k = y_t.shape
    n = c_rows * S5_CHUNK
    tm = min(1024, n)
    return pl.pallas_call(
        _glu_kernel,
        grid=(n // tm,),
        in_specs=[pl.BlockSpec((S5_CHUNK, tm // S5_CHUNK, k), lambda i: (0, i, 0)),
                  pl.BlockSpec((None, k, S5_WIDTH), lambda i: (layer, 0, 0)),
                  pl.BlockSpec((None, k, S5_WIDTH), lambda i: (layer, 0, 1))],
        out_specs=pl.BlockSpec((tm, S5_WIDTH), lambda i: (i, 0)),
        out_shape=jax.ShapeDtypeStruct((n, S5_WIDTH), BF16),
        compiler_params=_cparams("parallel"),
        name="s5_glu",
    )(y_t, w_glu, w_glu)


def _ssdconv_kernel(x_ref, xp_ref, xn_ref, w_ref, b_ref, o_ref, *, nt):
    i = pl.program_id(0)
    x = x_ref[...].astype(F32)
    tr = x.shape[0]
    prev = jnp.where(i > 0, xp_ref[...].astype(F32)[15:16, :], 0.0)
    nxt = jnp.where(i < nt - 1, xn_ref[...].astype(F32)[0:1, :], 0.0)
    ri = lax.broadcasted_iota(jnp.int32, x.shape, 0)
    x_m1 = jnp.where(ri == 0, prev, pltpu.roll(x, 1, 0))
    x_p1 = jnp.where(ri == tr - 1, nxt, pltpu.roll(x, tr - 1, 0))
    y = w_ref[0:1, :] * x_m1 + w_ref[1:2, :] * x + w_ref[2:3, :] * x_p1 + b_ref[...]
    o_ref[...] = _silu(y).astype(BF16)


def _ssd_conv(pm, conv_w, conv_b):
    n = pm.shape[0]
    tr = min(1024, n)
    tc = 1024
    nt = n // tr
    hb = tr // 16
    return pl.pallas_call(
        functools.partial(_ssdconv_kernel, nt=nt),
        grid=(nt, SSD_CONV_CH // tc),
        in_specs=[pl.BlockSpec((tr, tc), lambda i, j: (i, j)),
                  pl.BlockSpec((16, tc), lambda i, j: (jnp.maximum(i * hb - 1, 0), j)),
                  pl.BlockSpec((16, tc), lambda i, j: (jnp.minimum((i + 1) * hb, n // 16 - 1), j)),
                  pl.BlockSpec((3, tc), lambda i, j: (0, j)),
                  pl.BlockSpec((1, tc), lambda i, j: (0, j))],
        out_specs=pl.BlockSpec((tr, tc), lambda i, j: (i, j)),
        out_shape=jax.ShapeDtypeStruct((n, SSD_CONV_CH), BF16),
        compiler_params=_cparams("parallel", "parallel"),
        name="ssd_conv",
    )(pm, pm, pm, conv_w, conv_b)


def _head_expand(v, mode):
    nh = 2 * SSD_HEADS
    rows = lax.broadcasted_iota(jnp.int32, (2 * nh, SSD_WIDTH), 0) & (nh - 1)
    head = lax.broadcasted_iota(jnp.int32, (2 * nh, SSD_WIDTH), 1) >> 6
    if mode == 'fwd':
        sel = rows == head
    elif mode == 'bwd':
        sel = rows == head + SSD_HEADS
    else:
        sel = (rows & (SSD_HEADS - 1)) == head
    onehot = jnp.where(sel, 1.0, 0.0).astype(BF16)
    hi = v.astype(BF16)
    lo = (v - hi.astype(F32)).astype(BF16)
    return jnp.dot(jnp.concatenate([hi, lo], axis=1), onehot, preferred_element_type=F32)


def _tri_masks(t):
    r = lax.broadcasted_iota(jnp.int32, (t, t), 0)
    c = lax.broadcasted_iota(jnp.int32, (t, t), 1)
    return r, c


SSD_STEP = 4 * SSD_CHUNK


def _ssd_fwd_kernel(xc_ref, dt_ref, dtt_ref, bias_ref, biast_ref, a_ref, at_ref, s0_ref,
                    y_ref, sfin_ref, s_ref, *, nc):
    t = SSD_CHUNK
    c = pl.program_id(0)

    @pl.when(c == 0)
    def _():
        s_ref[...] = s0_ref[...]

    r, cc = _tri_masks(t)
    lo_incl = jnp.where(r >= cc, 1.0, 0.0)
    up_incl = jnp.where(r <= cc, 1.0, 0.0)
    lo = r > cc
    diag = r == cc
    lane = lax.broadcasted_iota(jnp.int32, (t, 128), 1)

    for sub in range(xc_ref.shape[0] // t):
        rs = slice(sub * t, (sub + 1) * t)
        dt = _softplus(dt_ref[rs, 0:2 * SSD_HEADS] + bias_ref[...])
        dtt = _softplus(dtt_ref[:, rs] + biast_ref[...])
        la = dt * a_ref[...]
        lat = dtt * at_ref[...]
        cum_lo = jnp.dot(lo_incl, la, preferred_element_type=F32, precision=HIGHEST)
        cum_up = jnp.dot(up_incl, la, preferred_element_type=F32, precision=HIGHEST)
        cumt_f = jnp.dot(lat, up_incl, preferred_element_type=F32, precision=HIGHEST)
        cumt_b = jnp.dot(lat, lo_incl, preferred_element_type=F32, precision=HIGHEST)

        end_f = cum_lo[t - 1:t, :]
        exp_f = _head_expand(jnp.exp(cum_lo), 'fwd')
        w_f = _head_expand(jnp.exp(end_f - cum_lo) * dt, 'fwd')

        for g in range(SSD_GROUPS):
            gs = slice(g * SSD_GW, (g + 1) * SSD_GW)
            bg = xc_ref[rs, SSD_WIDTH + g * SSD_STATE:SSD_WIDTH + (g + 1) * SSD_STATE]
            cg = xc_ref[rs, SSD_WIDTH + (SSD_GROUPS + g) * SSD_STATE:SSD_WIDTH + (SSD_GROUPS + g + 1) * SSD_STATE]
            cb = lax.dot_general(cg, bg, (((1,), (1,)), ((), ())), preferred_element_type=F32)
            cb = jnp.where(diag, 0.0, cb)
            bt = bg.astype(F32).T.astype(BF16)
            xg = xc_ref[rs, gs]
            sg = s_ref[g]
            y_int = exp_f[:, gs] * jnp.dot(cg, sg.astype(BF16), preferred_element_type=F32)
            xw = (xg.astype(F32) * w_f[:, gs]).astype(BF16)
            s_ref[g] = sg * exp_f[t - 1:t, gs] + jnp.dot(bt, xw, preferred_element_type=F32)
            for p in range(SSD_GW // 128):
                xp = xg[:, p * 128:(p + 1) * 128]
                acc = y_int[:, p * 128:(p + 1) * 128]
                for q in range(2):
                    hh = g * (SSD_HEADS // SSD_GROUPS) + 2 * p + q
                    hb = SSD_HEADS + hh
                    csel = jnp.where(lo, cum_lo[:, hh:hh + 1], cum_up[:, hb:hb + 1])
                    rsel = jnp.where(lo, cumt_f[hh:hh + 1, :], cumt_b[hb:hb + 1, :])
                    dsel = jnp.where(lo, dtt[hh:hh + 1, :], dtt[hb:hb + 1, :])
                    sc = (cb * jnp.exp(csel - rsel) * dsel).astype(BF16)
                    xh = jnp.where((lane < 64) if q == 0 else (lane >= 64), xp, jnp.zeros_like(xp))
                    acc = acc + jnp.dot(sc, xh, preferred_element_type=F32)
                y_ref[rs, g * SSD_GW + p * 128:g * SSD_GW + (p + 1) * 128] = acc

    @pl.when(c == nc - 1)
    def _():
        sfin_ref[...] = s_ref[...]


def _ssd_bwd_kernel(xc_ref, dt_ref, z_ref, yp_ref, bias_ref, a_ref, dvec_ref, nw_ref, s0_ref,
                    o_ref, sfin_ref, s_ref, *, nc):
    t = SSD_CHUNK
    c = pl.program_id(0)

    @pl.when(c == 0)
    def _():
        s_ref[...] = s0_ref[...]

    ts = t
    r, cc = _tri_masks(ts)
    up_incl = jnp.where(r <= cc, 1.0, 0.0)
    grp = (lax.broadcasted_iota(jnp.int32, (ts, 2 * SSD_HEADS), 1) & (SSD_HEADS - 1)) >> 3

    for sub in range(xc_ref.shape[0] // ts - 1, -1, -1):
        rs = slice(sub * ts, (sub + 1) * ts)
        dt = _softplus(dt_ref[rs, 0:2 * SSD_HEADS] + bias_ref[...])
        la = dt * a_ref[...]
        cum_up = jnp.dot(up_incl, la, preferred_element_type=F32, precision=HIGHEST)
        exp_b = _head_expand(jnp.exp(cum_up), 'bwd')
        w_b = _head_expand(jnp.exp(cum_up[0:1, :] - cum_up) * dt, 'bwd')

        bgs = [xc_ref[rs, SSD_WIDTH + g * SSD_STATE:SSD_WIDTH + (g + 1) * SSD_STATE] for g in range(SSD_GROUPS)]
        cgs = [xc_ref[rs, SSD_WIDTH + (SSD_GROUPS + g) * SSD_STATE:SSD_WIDTH + (SSD_GROUPS + g + 1) * SSD_STATE]
               for g in range(SSD_GROUPS)]
        cbd = jnp.zeros((ts, 2 * SSD_HEADS), F32)
        for g in range(SSD_GROUPS):
            dg = jnp.sum(cgs[g].astype(F32) * bgs[g].astype(F32), axis=-1, keepdims=True)
            cbd = jnp.where(grp == g, dg, cbd)
        coef = _head_expand(dt * cbd, 'both') + dvec_ref[...]

        for g in range(SSD_GROUPS):
            gs = slice(g * SSD_GW, (g + 1) * SSD_GW)
            bt = bgs[g].astype(F32).T.astype(BF16)
            xg = xc_ref[rs, gs].astype(F32)
            sg = s_ref[g]
            y_int = exp_b[:, gs] * jnp.dot(cgs[g], sg.astype(BF16), preferred_element_type=F32)
            xw = (xg * w_b[:, gs]).astype(BF16)
            s_ref[g] = sg * exp_b[0:1, gs] + jnp.dot(bt, xw, preferred_element_type=F32)
            y = yp_ref[rs, gs] + y_int + xg * coef[:, gs]
            hcur = y * _silu(z_ref[rs, gs].astype(F32))
            ms = jnp.mean(hcur * hcur, axis=-1, keepdims=True)
            o_ref[rs, gs] = (hcur * lax.rsqrt(ms + RMS_EPS) * nw_ref[:, gs]).astype(BF16)

    @pl.when(c == nc - 1)
    def _():
        sfin_ref[...] = s_ref[...]


def _ssd_scan(xc, dt_raw, dt_t, pm, prm, s0_f, s0_b):
    n = xc.shape[0]
    t = min(SSD_STEP, n)
    nc = n // t
    full = lambda a: pl.BlockSpec(a.shape, lambda c: (0,) * a.ndim)
    s_shape = (SSD_GROUPS, SSD_STATE, SSD_GW)
    s_spec = pl.BlockSpec(s_shape, lambda c: (0, 0, 0))
    y_part, sfin_f = pl.pallas_call(
        functools.partial(_ssd_fwd_kernel, nc=nc),
        grid=(nc,),
        in_specs=[pl.BlockSpec((t, SSD_CONV_CH), lambda c: (c, 0)),
                  pl.BlockSpec((t, DT_PAD), lambda c: (c, 0)),
                  pl.BlockSpec((2 * SSD_HEADS, t), lambda c: (0, c)),
                  full(prm['bias']), full(prm['bias_t']), full(prm['a']), full(prm['a_t']), s_spec],
        out_specs=[pl.BlockSpec((t, SSD_WIDTH), lambda c: (c, 0)), s_spec],
        out_shape=[jax.ShapeDtypeStruct((n, SSD_WIDTH), F32), jax.ShapeDtypeStruct(s_shape, F32)],
        scratch_shapes=[pltpu.VMEM(s_shape, F32)],
        compiler_params=_cparams("arbitrary"),
        name="ssd_fwd",
    )(xc, dt_raw, dt_t, prm['bias'], prm['bias_t'], prm['a'], prm['a_t'], s0_f)
    rev = lambda c: (nc - 1 - c, 0)
    z_blk = PM_Z // SSD_WIDTH
    yd, sfin_b = pl.pallas_call(
        functools.partial(_ssd_bwd_kernel, nc=nc),
        grid=(nc,),
        in_specs=[pl.BlockSpec((t, SSD_CONV_CH), rev),
                  pl.BlockSpec((t, DT_PAD), rev),
                  pl.BlockSpec((t, SSD_WIDTH), lambda c: (nc - 1 - c, z_blk)),
                  pl.BlockSpec((t, SSD_WIDTH), rev),
                  full(prm['bias']), full(prm['a']), full(prm['dvec']), full(prm['norm_w']), s_spec],
        out_specs=[pl.BlockSpec((t, SSD_WIDTH), rev), s_spec],
        out_shape=[jax.ShapeDtypeStruct((n, SSD_WIDTH), BF16), jax.ShapeDtypeStruct(s_shape, F32)],
        scratch_shapes=[pltpu.VMEM(s_shape, F32)],
        compiler_params=_cparams("arbitrary"),
        name="ssd_bwd",
    )(xc, dt_raw, pm, y_part, prm['bias'], prm['a'], prm['dvec'], prm['norm_w'], s0_b)
    return yd, sfin_f, sfin_b


def _merge_kernel(y5_ref, yd_ref, w5_ref, wd_ref, g5_ref, gd_ref, o_ref):
    a = jnp.dot(y5_ref[...], w5_ref[...], preferred_element_type=F32)
    b = jnp.dot(yd_ref[...], wd_ref[...], preferred_element_type=F32)
    o = _sigmoid(g5_ref[...].astype(F32)) * a + _sigmoid(gd_ref[...].astype(F32)) * b
    o_ref[...] = o.astype(BF16)


def _merge(y5, yd, w5, wd, pm, layer):
    n = y5.shape[0]
    tm = min(1024, n)
    tn = 1024
    g5_blk, gd_blk = PM_G5 // tn, PM_GD // tn
    return pl.pallas_call(
        _merge_kernel,
        grid=(n // tm, D_MODEL // tn),
        in_specs=[pl.BlockSpec((tm, S5_WIDTH), lambda i, j: (i, 0)),
                  pl.BlockSpec((tm, SSD_WIDTH), lambda i, j: (i, 0)),
                  pl.BlockSpec((None, S5_WIDTH, tn), lambda i, j: (layer, 0, j)),
                  pl.BlockSpec((None, SSD_WIDTH, tn), lambda i, j: (layer, 0, j)),
                  pl.BlockSpec((tm, tn), lambda i, j: (i, g5_blk + j)),
                  pl.BlockSpec((tm, tn), lambda i, j: (i, gd_blk + j))],
        out_specs=pl.BlockSpec((tm, tn), lambda i, j: (i, j)),
        out_shape=jax.ShapeDtypeStruct((n, D_MODEL), BF16),
        compiler_params=_cparams("parallel", "parallel"),
        name="merge",
    )(y5, yd, w5, wd, pm, pm)


def _proj_ln_kernel(x_ref, w_ref, h_ref, m_ref, lg_ref, lb_ref, o_ref, acc_ref, *, row, g_off, nk, alpha):
    k = pl.program_id(1)
    part = jnp.dot(x_ref[...], w_ref[...], preferred_element_type=F32)

    @pl.when(k == 0)
    def _():
        acc_ref[...] = part

    @pl.when(k > 0)
    def _():
        acc_ref[...] += part

    @pl.when(k == nk - 1)
    def _():
        d = h_ref.shape[1]
        gate = m_ref[row:row + 1, g_off:g_off + d]
        o_ref[...] = _layer_norm(alpha * h_ref[...] + gate * acc_ref[...], lg_ref[...], lb_ref[...])


def _proj_ln(x, w, layer, h, m, ln_g, ln_b, *, row, g_off, tk, alpha):
    n, kdim = x.shape
    d = h.shape[1]
    tm = min(512, n)
    nk = kdim // tk
    return pl.pallas_call(
        functools.partial(_proj_ln_kernel, row=row, g_off=g_off, nk=nk, alpha=alpha),
        grid=(n // tm, nk),
        in_specs=[pl.BlockSpec((tm, tk), lambda i, k: (i, k)),
                  pl.BlockSpec((None, tk, d), lambda i, k: (layer, k, 0)),
                  pl.BlockSpec((tm, d), lambda i, k: (i, 0)),
                  pl.BlockSpec(m.shape, lambda i, k: (0, 0)),
                  pl.BlockSpec((1, d), lambda i, k: (0, 0)),
                  pl.BlockSpec((1, d), lambda i, k: (0, 0))],
        out_specs=pl.BlockSpec((tm, d), lambda i, k: (i, 0)),
        out_shape=jax.ShapeDtypeStruct((n, d), F32),
        scratch_shapes=[pltpu.VMEM((tm, d), F32)],
        compiler_params=_cparams("parallel", "arbitrary"),
        name="proj_ln",
    )(x, w, h, m, ln_g, ln_b)


FFN_HALO = 128


def _ffnconv_kernel(g_ref, gp_ref, gn_ref, v_ref, w_ref, b_ref, o_ref, ext_ref, *, nt, grid_mode):
    i = pl.program_id(0)
    tr = g_ref.shape[0]
    hl = FFN_HALO
    ext_ref[0:hl, :] = jnp.where(i > 0, gp_ref[...].astype(F32), 0.0)
    ext_ref[hl:hl + tr, :] = g_ref[...].astype(F32)
    ext_ref[hl + tr:hl + tr + hl, :] = jnp.where(i < nt - 1, gn_ref[...].astype(F32), 0.0)

    def tap(dy, dx):
        wrow = (dy + 1) * 3 + (dx + 1)
        return w_ref[wrow:wrow + 1, :] * ext_ref[pl.ds(hl + GRID_W * dy + dx, tr), :]

    if grid_mode:
        col = lax.broadcasted_iota(jnp.int32, (tr, g_ref.shape[1]), 0) & (GRID_W - 1)
        acc_c = tap(-1, 0) + tap(0, 0) + tap(1, 0)
        acc_l = tap(-1, -1) + tap(0, -1) + tap(1, -1)
        acc_r = tap(-1, 1) + tap(0, 1) + tap(1, 1)
        y = acc_c + jnp.where(col != 0, acc_l, 0.0) + jnp.where(col != GRID_W - 1, acc_r, 0.0)
    else:
        y = tap(0, -1) + tap(0, 0) + tap(0, 1)
    y = y + b_ref[...]
    o_ref[...] = (_silu(y) * v_ref[...].astype(F32)).astype(BF16)


def _ffn_conv(up, conv_w9, conv_b, *, grid_mode):
    n = up.shape[0]
    tr = min(1024, n)
    tc = 512
    nt = n // tr
    hb = tr // FFN_HALO
    ncb = D_FF // tc
    return pl.pallas_call(
        functools.partial(_ffnconv_kernel, nt=nt, grid_mode=grid_mode),
        grid=(nt, ncb),
        in_specs=[pl.BlockSpec((tr, tc), lambda i, j: (i, j)),
                  pl.BlockSpec((FFN_HALO, tc), lambda i, j: (jnp.maximum(i * hb - 1, 0), j)),
                  pl.BlockSpec((FFN_HALO, tc), lambda i, j: (jnp.minimum((i + 1) * hb, n // FFN_HALO - 1), j)),
                  pl.BlockSpec((tr, tc), lambda i, j: (i, ncb + j)),
                  pl.BlockSpec((9, tc), lambda i, j: (0, j)),
                  pl.BlockSpec((1, tc), lambda i, j: (0, j))],
        out_specs=pl.BlockSpec((tr, tc), lambda i, j: (i, j)),
        out_shape=jax.ShapeDtypeStruct((n, D_FF), BF16),
        scratch_shapes=[pltpu.VMEM((tr + 2 * FFN_HALO, tc), F32)],
        compiler_params=_cparams("parallel", "parallel"),
        name="ffn_conv",
    )(up, up, up, up, conv_w9, conv_b)


def _mixer(h, m, row, sw, lw, layer, q, s5_init, ssd_init):
    pm, dt_raw, u_t = _modmm(h, m, sw['w_main'], layer, row=row, sh_off=0, wdt=sw['w_dt'])
    interleave = u_t.shape[1] % 64 == 0
    if not interleave:
        u_t = _s5_segment_rows(u_t, False)
    y5t, s5_fin = _s5_scan(u_t, sw['s5'], layer, q, s5_init, interleave=interleave)
    if not interleave:
        y5t = _s5_segment_rows(y5t, True)
    y5 = _s5_glu(y5t, sw['s5_w_glu'], layer)
    xc = _ssd_conv(pm, lw['ssd_conv_w'], lw['ssd_conv_b'])
    dt_t = dt_raw[:, 0:2 * SSD_HEADS].T
    yd, sf, sb = _ssd_scan(xc, dt_raw, dt_t, pm, lw['ssd'], ssd_init[0], ssd_init[1])
    return pm, y5, yd, s5_fin, (sf, sb)


def _rest_of_layer(h, m, row, sw, lw, layer, pm, y5, yd, alpha, *, grid_mode):
    d = D_MODEL
    merged = _merge(y5, yd, sw['s5_w_proj'], sw['ssd_w_proj'], pm, layer)
    h1 = _proj_ln(merged, sw['w_out'], layer, h, m, lw['ln1_g'], lw['ln1_b'], row=row, g_off=2 * d, tk=d, alpha=alpha)
    up = _modmm(h1, m, sw['w_up'], layer, row=row, sh_off=3 * d)
    act = _ffn_conv(up, lw['ffn_conv_w'], lw['ffn_conv_b'], grid_mode=grid_mode)
    return _proj_ln(act, sw['w_down'], layer, h1, m, lw['ln2_g'], lw['ln2_b'], row=row, g_off=5 * d, tk=D_FF // 4,
                    alpha=alpha)


def kernel(x, c, ctx, c_ctx, w_ada, b_ada, w_in, s5_lam_re, s5_lam_im, s5_log_step, s5_b_re, s5_b_im, s5_c_re, s5_c_im, s5_d, s5_w_glu, s5_w_proj, ssd_conv_w, ssd_conv_b, ssd_dt_bias, ssd_a_log, ssd_d, ssd_norm_w, ssd_w_proj, w_out, ln1_g, ln1_b, w_up, ffn_conv_w, ffn_conv_b, w_down, ln2_g, ln2_b):
    depth = w_ada.shape[0]
    alpha = float((2 * depth) ** 0.25)
    n_lat, n_ctx = x.shape[1], ctx.shape[1]
    h_lat, h_ctx = x[0], ctx[0]

    cc = jnp.zeros((8, D_MODEL), F32).at[0].set(c[0]).at[1].set(c_ctx)
    mods = _adaln(cc, w_ada, b_ada)

    seg = (n_lat // (8 * S5_CHUNK), n_ctx // (8 * S5_CHUNK))
    s5_ops, (q_lat, q_ctx) = _s5_operators(s5_lam_re, s5_lam_im, s5_log_step, s5_b_re, s5_b_im,
                                           s5_c_re, s5_c_im, s5_d, seg)

    w_main, w_dt = _repack_w_in(w_in)
    sw = {
        'w_main': w_main, 'w_dt': w_dt,
        's5': s5_ops, 's5_w_glu': s5_w_glu.astype(BF16), 's5_w_proj': s5_w_proj.astype(BF16),
        'ssd_w_proj': ssd_w_proj.astype(BF16), 'w_out': w_out.astype(BF16),
        'w_up': w_up.astype(BF16), 'w_down': w_down.astype(BF16),
    }
    a64 = (-jnp.exp(ssd_a_log.astype(F32))).reshape(depth, 1, 2 * SSD_HEADS)
    bias64 = ssd_dt_bias.astype(F32).reshape(depth, 1, 2 * SSD_HEADS)
    dvec = jnp.repeat(ssd_d.astype(F32), SSD_HEAD_DIM, axis=1).reshape(depth, 1, SSD_WIDTH)

    s5_zero = jnp.zeros((S5_GROUPS, 8, 128), F32)
    ssd_zero = jnp.zeros((SSD_GROUPS, SSD_STATE, SSD_GW), F32)

    for i in range(depth):
        last = i == depth - 1
        lw = {
            'ssd_conv_w': ssd_conv_w[i], 'ssd_conv_b': ssd_conv_b[i].reshape(1, SSD_CONV_CH),
            'ssd': {'bias': bias64[i], 'bias_t': bias64[i].reshape(2 * SSD_HEADS, 1),
                    'a': a64[i], 'a_t': a64[i].reshape(2 * SSD_HEADS, 1),
                    'dvec': dvec[i], 'norm_w': ssd_norm_w[i].astype(F32).reshape(1, SSD_WIDTH)},
            'ln1_g': ln1_g[i].reshape(1, D_MODEL), 'ln1_b': ln1_b[i].reshape(1, D_MODEL),
            'ffn_conv_w': ffn_conv_w[i].reshape(9, D_FF), 'ffn_conv_b': ffn_conv_b[i].reshape(1, D_FF),
            'ln2_g': ln2_g[i].reshape(1, D_MODEL), 'ln2_b': ln2_b[i].reshape(1, D_MODEL),
        }
        m = mods[i]
        pm_c, y5_c, yd_c, s5_fin, ssd_fin = _mixer(h_ctx, m, 1, sw, lw, i, q_ctx, s5_zero, (ssd_zero, ssd_zero))
        pm_l, y5_l, yd_l, _, _ = _mixer(h_lat, m, 0, sw, lw, i, q_lat, s5_fin, ssd_fin)
        h_lat = _rest_of_layer(h_lat, m, 0, sw, lw, i, pm_l, y5_l, yd_l, alpha, grid_mode=True)
        if not last:
            h_ctx = _rest_of_layer(h_ctx, m, 1, sw, lw, i, pm_c, y5_c, yd_c, alpha, grid_mode=False)
    return h_lat[None]
```
